```python
import math
import jax, jax.numpy as jnp
from jax import lax
import numpy as np

D_MODEL = 1024
BATCH = 8
SEQ = 2048
DEPTH = 4
DEC_BATCH = 128
DEC_SEQ = 1
PAST_LEN = 2048
PAGE_SIZE = 128

N_MIXERS = 3
N_ATTN = (DEPTH + 2) // 3
N_RWKV = (DEPTH + 1) // 3
N_CONV = DEPTH // 3

ATTN_HEAD_DIM = 64
ATTN_HEADS = D_MODEL // (2 * ATTN_HEAD_DIM)
ATTN_VDIM = 2 * ATTN_HEAD_DIM
ROPE_THETA = 10000.0
Q_BLOCK = 128
NEG_INF = -1e30

RWKV_HEAD = 64
RWKV_HEADS = D_MODEL // RWKV_HEAD
DECAY_LORA = 64
AAA_LORA = 64
GATE_LORA = 128
RWKV_GN_EPS = 64e-5

CONV_WIDTH = 31
CONV_BUF = CONV_WIDTH - 1
LN_EPS = 1e-5

FFN_HIDDEN = 4 * D_MODEL
NORM_EPS = 1e-6

kernel_name = 'hybrid_diffattn_rwkv7_conformer_decode_step'


def rmsnorm(x, g):
    xf = x.astype(jnp.float32)
    y = xf * lax.rsqrt(jnp.mean(xf * xf, axis=-1, keepdims=True) + NORM_EPS)
    return (y * g.astype(jnp.float32)).astype(x.dtype)


def lambda_init(layer_idx):
    return 0.8 - 0.6 * math.exp(-0.3 * layer_idx)


def rope(x, pos):
    half = ATTN_HEAD_DIM // 2
    inv = ROPE_THETA ** (-jnp.arange(half, dtype=jnp.float32) / half)
    ang = pos.astype(jnp.float32)[:, None] * inv[None, :]
    cos = jnp.cos(ang)[None, :, None, None, :]
    sin = jnp.sin(ang)[None, :, None, None, :]
    xf = x.astype(jnp.float32)
    x1, x2 = xf[..., :half], xf[..., half:]
    return jnp.concatenate([x1 * cos - x2 * sin, x2 * cos + x1 * sin], axis=-1).astype(x.dtype)


def diff_attn_block(q, k, v, q_pos, k_pos, lam):
    s = jnp.einsum('bqhcd,bkhcd->bhcqk', q.astype(jnp.float32), k.astype(jnp.float32)) * (ATTN_HEAD_DIM ** -0.5)
    mask = k_pos[None, :] <= q_pos[:, None]
    p = jax.nn.softmax(jnp.where(mask, s, NEG_INF), axis=-1)
    a = p[:, :, 0] - lam * p[:, :, 1]
    return jnp.einsum('bhqk,bkhe->bqhe', a, v.astype(jnp.float32))


def blocked_diff_attention(q, k, v, q_pos, k_pos, lam):
    b, t = q.shape[0], q.shape[1]
    if t <= Q_BLOCK:
        return diff_attn_block(q, k, v, q_pos, k_pos, lam)
    nb = t // Q_BLOCK
    qb = q.reshape(b, nb, Q_BLOCK, ATTN_HEADS, 2, ATTN_HEAD_DIM).transpose(1, 0, 2, 3, 4, 5)
    pb = q_pos.reshape(nb, Q_BLOCK)
    out = lax.map(lambda a: diff_attn_block(a[0], k, v, a[1], k_pos, lam), (qb, pb))
    return out.transpose(1, 0, 2, 3, 4).reshape(b, t, ATTN_HEADS, ATTN_VDIM)


def diff_attention(h, pos0, k_past, v_past, w_qkv, w_o, lam_vec, subln, lam_init):
    b, t, _ = h.shape
    q, k, v = jnp.split(h @ w_qkv, 3, axis=-1)
    q_pos = pos0 + jnp.arange(t, dtype=jnp.int32)
    q = rope(q.reshape(b, t, ATTN_HEADS, 2, ATTN_HEAD_DIM), q_pos)
    k = rope(k.reshape(b, t, ATTN_HEADS, 2, ATTN_HEAD_DIM), q_pos)
    v = v.reshape(b, t, ATTN_HEADS, ATTN_VDIM)
    k_rows = k.reshape(b, t, ATTN_HEADS, ATTN_VDIM)
    if k_past is None:
        keys, vals, k_pos = k, v, q_pos
    else:
        p_len = k_past.shape[1]
        keys = jnp.concatenate([k_past.reshape(b, p_len, ATTN_HEADS, 2, ATTN_HEAD_DIM).astype(k.dtype), k], axis=1)
        vals = jnp.concatenate([v_past.astype(v.dtype), v], axis=1)
        k_pos = jnp.arange(p_len + t, dtype=jnp.int32)
    lv = lam_vec.astype(jnp.float32)
    lam = jnp.exp(jnp.sum(lv[0] * lv[1])) - jnp.exp(jnp.sum(lv[2] * lv[3])) + lam_init
    o = blocked_diff_attention(q, keys, vals, q_pos, k_pos, lam)
    o = o * lax.rsqrt(jnp.mean(o * o, axis=-1, keepdims=True) + LN_EPS)
    o = o * subln.astype(jnp.float32) * (1.0 - lam_init)
    out = o.reshape(b, t, D_MODEL).astype(h.dtype) @ w_o
    return out, k_rows, v


def wkv7_scan(s0, r, w, k, v, kk, a):
    def step(s, inp):
        r_t, w_t, k_t, v_t, kk_t, a_t = inp
        sa = jnp.einsum('bhij,bhj->bhi', s, -kk_t)
        s = s * w_t[:, :, None, :] + sa[..., None] * (kk_t * a_t)[:, :, None, :] + v_t[..., None] * k_t[:, :, None, :]
        y = jnp.einsum('bhij,bhj->bhi', s, r_t)
        return s, y
    xs = tuple(z.transpose(1, 0, 2, 3) for z in (r, w, k, v, kk, a))
    s_t, ys = lax.scan(step, s0.astype(jnp.float32), xs)
    return ys.transpose(1, 0, 2, 3), s_t


def rwkv7_time_mix(h, shift_prev, s0, mix, w0, w1, w2, a0, a1, a2, g1, g2,
                   k_k, k_a, r_k, wr, wk, wv, wo, ln_w, ln_b):
    b, t, _ = h.shape
    prev = jnp.concatenate([shift_prev[:, None, :].astype(h.dtype), h[:, :-1]], axis=1)
    xx = prev - h
    xr, xw, xk, xv, xa, xg = [h + xx * mix[c] for c in range(6)]
    f32 = jnp.float32
    r = (xr @ wr).astype(f32)
    k = (xk @ wk).astype(f32)
    v = (xv @ wv).astype(f32)
    w_log = -jax.nn.softplus(-(w0 + jnp.tanh(xw @ w1) @ w2).astype(f32)) - 0.5
    decay = jnp.exp(-jnp.exp(w_log))
    a = jax.nn.sigmoid((a0 + (xa @ a1) @ a2).astype(f32))
    g = (jax.nn.sigmoid(xg @ g1) @ g2).astype(f32)
    hd = lambda z: z.reshape(b, t, RWKV_HEADS, RWKV_HEAD)
    kk = hd(k * k_k.astype(f32))
    kk = kk / jnp.maximum(jnp.sqrt(jnp.sum(kk * kk, axis=-1, keepdims=True)), 1e-12)
    k = k * (1.0 + (a - 1.0) * k_a.astype(f32))
    r, decay, k, v, a = hd(r), hd(decay), hd(k), hd(v), hd(a)
    y, s_t = wkv7_scan(s0, r, decay, k, v, kk, a)
    mu = jnp.mean(y, axis=-1, keepdims=True)
    var = jnp.mean((y - mu) ** 2, axis=-1, keepdims=True)
    yn = ((y - mu) * lax.rsqrt(var + RWKV_GN_EPS)).reshape(b, t, D_MODEL)
    yn = yn * ln_w.astype(f32) + ln_b.astype(f32)
    bonus = jnp.sum(r * k * r_k.astype(f32), axis=-1, keepdims=True) * v
    y = yn + bonus.reshape(b, t, D_MODEL)
    out = (y * g).astype(h.dtype) @ wo
    return out, s_t.astype(s0.dtype), h[:, -1]


def conformer_conv(h, buf, w1, b1, dw, dw_b, ln_w, ln_b, w2, b2):
    u = h @ w1 + b1
    u = u[..., :D_MODEL] * jax.nn.sigmoid(u[..., D_MODEL:])
    full = jnp.concatenate([buf.astype(u.dtype), u], axis=1)
    y = lax.conv_general_dilated(full, dw[:, None, :].astype(u.dtype), window_strides=(1,), padding='VALID',
                                 dimension_numbers=('NWC', 'WIO', 'NWC'), feature_group_count=D_MODEL) + dw_b
    yf = y.astype(jnp.float32)
    mu = jnp.mean(yf, axis=-1, keepdims=True)
    var = jnp.mean((yf - mu) ** 2, axis=-1, keepdims=True)
    yn = (yf - mu) * lax.rsqrt(var + LN_EPS) * ln_w.astype(jnp.float32) + ln_b.astype(jnp.float32)
    yn = jax.nn.silu(yn).astype(h.dtype)
    return yn @ w2 + b2, full[:, -CONV_BUF:]


def squared_relu_mlp(h, w_up, w_down):
    u = jax.nn.relu(h @ w_up)
    return (u * u) @ w_down


def setup_inputs(seed: int = 0) -> dict:
    key = jax.random.key(seed)
    ks = iter(jax.random.split(key, 64))
    nrm = lambda shape, scale: jax.random.normal(next(ks), shape, jnp.float32) * scale
    uni = lambda shape, lo, hi: jax.random.uniform(next(ks), shape, jnp.float32, lo, hi)
    D = D_MODEL
    n_pages = PAST_LEN // PAGE_SIZE
    n_used = DEC_BATCH * n_pages
    n_pool = n_used + n_used // 4
    page_table = jax.random.permutation(next(ks), n_pool)[:n_used].reshape(DEC_BATCH, n_pages).astype(jnp.int32)
    return {
        'x_prompt': nrm((BATCH, SEQ, D), 1.0),
        'x_sample': nrm((DEC_BATCH, DEC_SEQ, D), 1.0),
        'cache_k': nrm((N_ATTN, n_pool, PAGE_SIZE, ATTN_HEADS, ATTN_VDIM), 1.0),
        'cache_v': nrm((N_ATTN, n_pool, PAGE_SIZE, ATTN_HEADS, ATTN_VDIM), 1.0),
        'page_table': page_table,
        'state_wkv': nrm((N_RWKV, DEC_BATCH, RWKV_HEADS, RWKV_HEAD, RWKV_HEAD), 0.1),
        'state_shift': nrm((N_RWKV, DEC_BATCH, D), 1.0),
        'state_conv': nrm((N_CONV, DEC_BATCH, CONV_BUF, D), 0.5),
        'norm_mix': 1.0 + nrm((DEPTH, D), 0.02),
        'norm_ffn': 1.0 + nrm((DEPTH, D), 0.02),
        'norm_final': 1.0 + nrm((D,), 0.02),
        'attn_w_qkv': nrm((N_ATTN, D, 3 * D), D ** -0.5),
        'attn_w_o': nrm((N_ATTN, D, D), D ** -0.5),
        'attn_lambda': nrm((N_ATTN, 4, ATTN_HEAD_DIM), 0.1),
        'attn_subln': 1.0 + nrm((N_ATTN, ATTN_VDIM), 0.02),
        'rwkv_mix': uni((N_RWKV, 6, D), 0.0, 1.0),
        'rwkv_w0': uni((N_RWKV, D), -6.0, -1.0),
        'rwkv_w1': nrm((N_RWKV, D, DECAY_LORA), D ** -0.5),
        'rwkv_w2': nrm((N_RWKV, DECAY_LORA, D), 0.1 * DECAY_LORA ** -0.5),
        'rwkv_a0': nrm((N_RWKV, D), 0.1),
        'rwkv_a1': nrm((N_RWKV, D, AAA_LORA), D ** -0.5),
        'rwkv_a2': nrm((N_RWKV, AAA_LORA, D), 0.1 * AAA_LORA ** -0.5),
        'rwkv_g1': nrm((N_RWKV, D, GATE_LORA), D ** -0.5),
        'rwkv_g2': nrm((N_RWKV, GATE_LORA, D), GATE_LORA ** -0.5),
        'rwkv_k_k': 0.85 + nrm((N_RWKV, D), 0.02),
        'rwkv_k_a': 1.0 + nrm((N_RWKV, D), 0.02),
        'rwkv_r_k': nrm((N_RWKV, RWKV_HEADS, RWKV_HEAD), 0.1),
        'rwkv_wr': nrm((N_RWKV, D, D), D ** -0.5),
        'rwkv_wk': nrm((N_RWKV, D, D), D ** -0.5),
        'rwkv_wv': nrm((N_RWKV, D, D), D ** -0.5),
        'rwkv_wo': nrm((N_RWKV, D, D), D ** -0.5),
        'rwkv_ln_w': 1.0 + nrm((N_RWKV, D), 0.02),
        'rwkv_ln_b': nrm((N_RWKV, D), 0.02),
        'conv_w1': nrm((N_CONV, D, 2 * D), D ** -0.5),
        'conv_b1': nrm((N_CONV, 2 * D), 0.02),
        'conv_dw': nrm((N_CONV, CONV_WIDTH, D), CONV_WIDTH ** -0.5),
        'conv_dw_b': nrm((N_CONV, D), 0.02),
        'conv_ln_w': 1.0 + nrm((N_CONV, D), 0.02),
        'conv_ln_b': nrm((N_CONV, D), 0.02),
        'conv_w2': nrm((N_CONV, D, D), D ** -0.5),
        'conv_b2': nrm((N_CONV, D), 0.02),
        'ffn_w_up': nrm((DEPTH, D, FFN_HIDDEN), D ** -0.5),
        'ffn_w_down': nrm((DEPTH, FFN_HIDDEN, D), FFN_HIDDEN ** -0.5),
    }


def reference(x_prompt, x_sample, cache_k, cache_v, page_table, state_wkv, state_shift, state_conv,
              norm_mix, norm_ffn, norm_final,
              attn_w_qkv, attn_w_o, attn_lambda, attn_subln,
              rwkv_mix, rwkv_w0, rwkv_w1, rwkv_w2, rwkv_a0, rwkv_a1, rwkv_a2, rwkv_g1, rwkv_g2,
              rwkv_k_k, rwkv_k_a, rwkv_r_k, rwkv_wr, rwkv_wk, rwkv_wv, rwkv_wo, rwkv_ln_w, rwkv_ln_b,
              conv_w1, conv_b1, conv_dw, conv_dw_b, conv_ln_w, conv_ln_b, conv_w2, conv_b2,
              ffn_w_up, ffn_w_down):
    xp, xs = x_prompt, x_sample
    bp, bs = x_prompt.shape[0], x_sample.shape[0]
    kp, vp, ks_, vs_, wkv_p, wkv_s, sh_p, sh_s, cv_p, cv_s = ([] for _ in range(10))
    for i in range(DEPTH):
        j = i // N_MIXERS
        kind = i % N_MIXERS
        hp = rmsnorm(xp, norm_mix[i])
        hs = rmsnorm(xs, norm_mix[i])
        if kind == 0:
            args = (attn_w_qkv[j], attn_w_o[j], attn_lambda[j], attn_subln[j], lambda_init(i))
            op, k_new_p, v_new_p = diff_attention(hp, 0, None, None, *args)
            k_past = cache_k[j, page_table].reshape(bs, PAST_LEN, ATTN_HEADS, ATTN_VDIM)
            v_past = cache_v[j, page_table].reshape(bs, PAST_LEN, ATTN_HEADS, ATTN_VDIM)
            os_, k_new_s, v_new_s = diff_attention(hs, PAST_LEN, k_past, v_past, *args)
            kp.append(k_new_p); vp.append(v_new_p); ks_.append(k_new_s); vs_.append(v_new_s)
        elif kind == 1:
            args = (rwkv_mix[j], rwkv_w0[j], rwkv_w1[j], rwkv_w2[j], rwkv_a0[j], rwkv_a1[j], rwkv_a2[j],
                    rwkv_g1[j], rwkv_g2[j], rwkv_k_k[j], rwkv_k_a[j], rwkv_r_k[j],
                    rwkv_wr[j], rwkv_wk[j], rwkv_wv[j], rwkv_wo[j], rwkv_ln_w[j], rwkv_ln_b[j])
            s0_p = jnp.zeros((bp, RWKV_HEADS, RWKV_HEAD, RWKV_HEAD), x_prompt.dtype)
            op, s_p, last_p = rwkv7_time_mix(hp, jnp.zeros((bp, D_MODEL), hp.dtype), s0_p, *args)
            os_, s_s, last_s = rwkv7_time_mix(hs, state_shift[j], state_wkv[j], *args)
            wkv_p.append(s_p); wkv_s.append(s_s); sh_p.append(last_p); sh_s.append(last_s)
        else:
            args = (conv_w1[j], conv_b1[j], conv_dw[j], conv_dw_b[j], conv_ln_w[j], conv_ln_b[j], conv_w2[j], conv_b2[j])
            op, buf_p = conformer_conv(hp, jnp.zeros((bp, CONV_BUF, D_MODEL), hp.dtype), *args)
            os_, buf_s = conformer_conv(hs, state_conv[j], *args)
            cv_p.append(buf_p); cv_s.append(buf_s)
        xp = xp + op
        xs = xs + os_
        xp = xp + squared_relu_mlp(rmsnorm(xp, norm_ffn[i]), ffn_w_up[i], ffn_w_down[i])
        xs = xs + squared_relu_mlp(rmsnorm(xs, norm_ffn[i]), ffn_w_up[i], ffn_w_down[i])
    y_prompt = rmsnorm(xp, norm_final)
    y_sample = rmsnorm(xs, norm_final)
    k_prompt = jnp.stack(kp)
    v_prompt = jnp.stack(vp)
    k_sample = jnp.stack(ks_)
    v_sample = jnp.stack(vs_)
    wkv_prompt = jnp.stack(wkv_p)
    wkv_sample = jnp.stack(wkv_s)
    shift_prompt = jnp.stack(sh_p)
    shift_sample = jnp.stack(sh_s)
    conv_prompt = jnp.stack(cv_p)
    conv_sample = jnp.stack(cv_s)
    return (y_prompt, y_sample, k_prompt, v_prompt, k_sample, v_sample, wkv_prompt, wkv_sample,
            shift_prompt, shift_sample, conv_prompt, conv_sample)
```

```python
import functools
import math

import jax
import jax.numpy as jnp
from jax import lax
from jax.experimental import pallas as pl
from jax.experimental.pallas import tpu as pltpu

D_MODEL = 1024
DEPTH = 4
N_MIXERS = 3
ATTN_HEAD_DIM = 64
ATTN_HEADS = 8
ATTN_VDIM = 128
ROPE_THETA = 10000.0
NEG_INF = -1e30
RWKV_HEAD = 64
RWKV_HEADS = 16
RWKV_GN_EPS = 64e-5
CONV_WIDTH = 31
CONV_BUF = CONV_WIDTH - 1
LN_EPS = 1e-5
FFN_HIDDEN = 4 * D_MODEL
NORM_EPS = 1e-6

LANES = 128
VMEM_LIMIT = 56 * 1024 * 1024
WKV_CHUNK = 64
WKV_REFINE_STEPS = 2

F32 = jnp.float32
BF16 = jnp.bfloat16
HIGHEST = lax.Precision.HIGHEST


def _cparams(n_axes):
    return pltpu.CompilerParams(dimension_semantics=("arbitrary",) * n_axes, vmem_limit_bytes=VMEM_LIMIT)


def _resident(shape):
    nd = len(shape)
    return pl.BlockSpec(shape, lambda *_: (0,) * nd, pipeline_mode=pl.Buffered(1))


def _dot(a, b):
    return jnp.dot(a, b, preferred_element_type=F32)


def _dot_nt(a, b, precision=None):
    return lax.dot_general(a, b, (((1,), (1,)), ((), ())), preferred_element_type=F32, precision=precision)


def _dot_tn(a, b, precision=None):
    return lax.dot_general(a, b, (((0,), (0,)), ((), ())), preferred_element_type=F32, precision=precision)


def _rms(x, g, eps):
    return x * lax.rsqrt(jnp.mean(x * x, axis=-1, keepdims=True) + eps) * g


def _row_tile(m, want):
    return want if m % want == 0 else m


def _ffn_kernel(x_ref, g_ref, wup_ref, wdn_ref, o_ref, *, chunk):
    x = x_ref[...]
    h = _rms(x, g_ref[...], NORM_EPS).astype(BF16)
    acc = x
    for c in range(FFN_HIDDEN // chunk):
        u = _dot(h, wup_ref[:, c * chunk:(c + 1) * chunk])
        u = jnp.maximum(u, 0.0)
        acc = acc + _dot((u * u).astype(BF16), wdn_ref[c * chunk:(c + 1) * chunk, :])
    o_ref[...] = acc


def _ffn(x, g, wup, wdn):
    m = x.shape[0]
    tm = _row_tile(m, 512)
    return pl.pallas_call(
        functools.partial(_ffn_kernel, chunk=1024),
        grid=(m // tm,),
        in_specs=[pl.BlockSpec((tm, D_MODEL), lambda i: (i, 0)), _resident((1, D_MODEL)),
                  _resident((D_MODEL, FFN_HIDDEN)), _resident((FFN_HIDDEN, D_MODEL))],
        out_specs=pl.BlockSpec((tm, D_MODEL), lambda i: (i, 0)),
        out_shape=jax.ShapeDtypeStruct((m, D_MODEL), F32),
        compiler_params=_cparams(1), name="ffn",
    )(x, g.reshape(1, D_MODEL), wup, wdn)


def _matmul_res_kernel(a_ref, w_ref, b_ref, r_ref, o_ref):
    o_ref[...] = r_ref[...] + _dot(a_ref[...].astype(BF16), w_ref[...]) + b_ref[...]


def _matmul_res(a, w, bias, res):
    m, k = a.shape
    n = w.shape[1]
    tm = _row_tile(m, 512)
    return pl.pallas_call(
        _matmul_res_kernel,
        grid=(m // tm,),
        in_specs=[pl.BlockSpec((tm, k), lambda i: (i, 0)), _resident((k, n)), _resident((1, n)),
                  pl.BlockSpec((tm, n), lambda i: (i, 0))],
        out_specs=pl.BlockSpec((tm, n), lambda i: (i, 0)),
        out_shape=jax.ShapeDtypeStruct((m, n), F32),
        compiler_params=_cparams(1), name="matmul_res",
    )(a, w, bias.reshape(1, n), res)


def _norm_kernel(x_ref, g_ref, o_ref):
    o_ref[...] = _rms(x_ref[...], g_ref[...], NORM_EPS)


def _final_norm(x, g):
    m = x.shape[0]
    tm = _row_tile(m, 1024)
    return pl.pallas_call(
        _norm_kernel,
        grid=(m // tm,),
        in_specs=[pl.BlockSpec((tm, D_MODEL), lambda i: (i, 0)), _resident((1, D_MODEL))],
        out_specs=pl.BlockSpec((tm, D_MODEL), lambda i: (i, 0)),
        out_shape=jax.ShapeDtypeStruct((m, D_MODEL), F32),
        compiler_params=_cparams(1), name="final_norm",
    )(x, g.reshape(1, D_MODEL))


def _rope_tables(pos):
    half = ATTN_HEAD_DIM // 2
    inv = ROPE_THETA ** (-jnp.arange(half, dtype=F32) / half)
    ang = pos.astype(F32)[:, None] * inv[None, :]
    cos, sin = jnp.cos(ang), jnp.sin(ang)
    zero = jnp.zeros_like(sin)
    cos_t = jnp.tile(cos, (1, 4))
    sin_lo = jnp.tile(jnp.concatenate([-sin, zero], axis=1), (1, 2))
    sin_hi = jnp.tile(jnp.concatenate([zero, sin], axis=1), (1, 2))
    return cos_t, sin_lo, sin_hi


def _qkv_kernel(x_ref, g_ref, w_ref, cos_ref, slo_ref, shi_ref, q_ref, k_ref, v_ref):
    h = _rms(x_ref[...], g_ref[...], NORM_EPS).astype(BF16)
    cos, slo, shi = cos_ref[...], slo_ref[...], shi_ref[...]

    def rope(z):
        return z * cos + pltpu.roll(z, LANES - 32, 1) * slo + pltpu.roll(z, 32, 1) * shi

    q = _dot(h, w_ref[:, 0:D_MODEL])
    for c in range(ATTN_HEADS):
        sl = slice(c * LANES, (c + 1) * LANES)
        q_ref[:, sl] = (rope(q[:, sl]) * (ATTN_HEAD_DIM ** -0.5)).astype(BF16)
    k = _dot(h, w_ref[:, D_MODEL:2 * D_MODEL])
    for c in range(ATTN_HEADS):
        sl = slice(c * LANES, (c + 1) * LANES)
        k_ref[:, sl] = rope(k[:, sl])
    v_ref[...] = _dot(h, w_ref[:, 2 * D_MODEL:3 * D_MODEL])


def _attn_qkv(x, g, wqkv, tables, seq_len):
    m = x.shape[0]
    tm = _row_tile(min(m, seq_len), 512)
    nt = seq_len // tm
    tab_spec = pl.BlockSpec((tm, LANES), lambda i: (i % nt, 0))
    row_spec = pl.BlockSpec((tm, D_MODEL), lambda i: (i, 0))
    return pl.pallas_call(
        _qkv_kernel,
        grid=(m // tm,),
        in_specs=[row_spec, _resident((1, D_MODEL)), _resident((D_MODEL, 3 * D_MODEL)), tab_spec, tab_spec, tab_spec],
        out_specs=[row_spec, row_spec, row_spec],
        out_shape=[jax.ShapeDtypeStruct((m, D_MODEL), BF16), jax.ShapeDtypeStruct((m, D_MODEL), F32),
                   jax.ShapeDtypeStruct((m, D_MODEL), F32)],
        compiler_params=_cparams(1), name="attn_qkv",
    )(x, g.reshape(1, D_MODEL), wqkv, *tables)


def _lambda_value(lam_ref, lam_init):
    lv = lam_ref[...]
    a = jnp.sum(lv[0:1] * lv[1:2], axis=1, keepdims=True)
    b = jnp.sum(lv[2:3] * lv[3:4], axis=1, keepdims=True)
    return jnp.exp(a) - jnp.exp(b) + lam_init


def _head_out(o1, o2, lam, subln, lam_init):
    o = o1 - lam * o2
    o = o * lax.rsqrt(jnp.mean(o * o, axis=-1, keepdims=True) + LN_EPS)
    return o * subln * (1.0 - lam_init)


def _attn_prompt_kernel(q_ref, k_ref, v_ref, lam_ref, sub_ref, o_ref, kb_ref, vb_ref, m_ref, l_ref, acc_ref,
                        *, tq, lam_init):
    i = pl.program_id(2)

    @pl.when(i == 0)
    def _():
        kb_ref[...] = k_ref[...].astype(BF16)
        vb_ref[...] = v_ref[...].astype(BF16)

    q = q_ref[...]
    lane = lax.broadcasted_iota(jnp.int32, (tq, LANES), 1)
    first = lane < ATTN_HEAD_DIM
    zero = jnp.zeros_like(q)
    qq = jnp.concatenate([jnp.where(first, q, zero), jnp.where(first, zero, q)], axis=0)

    m_ref[...] = jnp.full((2 * tq, 1), NEG_INF, F32)
    l_ref[...] = jnp.zeros((2 * tq, 1), F32)
    acc_ref[...] = jnp.zeros((2 * tq, LANES), F32)

    def block(j, masked):
        off = pl.multiple_of(j * tq, tq)
        s = _dot_nt(qq, kb_ref[pl.ds(off, tq), :])
        if masked:
            r = lax.broadcasted_iota(jnp.int32, (2 * tq, tq), 0)
            c = lax.broadcasted_iota(jnp.int32, (2 * tq, tq), 1)
            r = jnp.where(r >= tq, r - tq, r)
            s = jnp.where(c <= r, s, NEG_INF)
        m_old = m_ref[...]
        m_new = jnp.maximum(m_old, jnp.max(s, axis=1, keepdims=True))
        alpha = jnp.exp(m_old - m_new)
        p = jnp.exp(s - m_new)
        l_ref[...] = alpha * l_ref[...] + jnp.sum(p, axis=1, keepdims=True)
        acc_ref[...] = alpha * acc_ref[...] + _dot(p.astype(BF16), vb_ref[pl.ds(off, tq), :])
        m_ref[...] = m_new

    def body(j, carry):
        block(j, False)
        return carry

    lax.fori_loop(0, i, body, 0)
    block(i, True)

    o = acc_ref[...] / l_ref[...]
    lam = _lambda_value(lam_ref, lam_init)
    o_ref[...] = _head_out(o[:tq], o[tq:], lam, sub_ref[...], lam_init).astype(BF16)


def _attn_prompt(q, k, v, lam_vec, subln, lam_init, batch, seq_len):
    tq = 256 if seq_len % 256 == 0 else seq_len
    nq = seq_len // tq
    q_spec = pl.BlockSpec((tq, LANES), lambda b, h, i: (b * nq + i, h))
    kv_spec = pl.BlockSpec((seq_len, LANES), lambda b, h, i: (b, h))
    return pl.pallas_call(
        functools.partial(_attn_prompt_kernel, tq=tq, lam_init=lam_init),
        grid=(batch, ATTN_HEADS, nq),
        in_specs=[q_spec, kv_spec, kv_spec, _resident((4, ATTN_HEAD_DIM)), _resident((1, ATTN_VDIM))],
        out_specs=q_spec,
        out_shape=jax.ShapeDtypeStruct((batch * seq_len, D_MODEL), BF16),
        scratch_shapes=[pltpu.VMEM((seq_len, LANES), BF16), pltpu.VMEM((seq_len, LANES), BF16),
                        pltpu.VMEM((2 * tq, 1), F32), pltpu.VMEM((2 * tq, 1), F32), pltpu.VMEM((2 * tq, LANES), F32)],
        compiler_params=_cparams(3), name="attn_prompt",
    )(q, k, v, lam_vec, subln.reshape(1, ATTN_VDIM))


def _attn_decode_kernel(pt_ref, q_ref, kn_ref, vn_ref, kc_ref, vc_ref, lam_ref, sub_ref, o_ref,
                        qm_ref, m_ref, l_ref, acc_ref, *, page, n_pages, lam_init):
    del pt_ref
    p = pl.program_id(1)
    sub = lax.broadcasted_iota(jnp.int32, (8, LANES), 0)
    lane = lax.broadcasted_iota(jnp.int32, (8, LANES), 1)
    sel = ((sub == 0) & (lane < ATTN_HEAD_DIM)) | ((sub == 1) & (lane >= ATTN_HEAD_DIM))

    @pl.when(p == 0)
    def _():
        q = q_ref[...].astype(F32)
        for h in range(ATTN_HEADS):
            qm_ref[h] = jnp.where(sel, jnp.broadcast_to(q[h:h + 1, :], (8, LANES)), 0.0)
        m_ref[...] = jnp.full(m_ref.shape, NEG_INF, F32)
        l_ref[...] = jnp.zeros(l_ref.shape, F32)
        acc_ref[...] = jnp.zeros(acc_ref.shape, F32)

    for h in range(ATTN_HEADS):
        kh = kc_ref[pl.ds(h, page, stride=ATTN_HEADS), :].astype(BF16)
        s = _dot_nt(qm_ref[h].astype(BF16), kh)
        m_old = m_ref[h]
        m_new = jnp.maximum(m_old, jnp.max(s, axis=1, keepdims=True))
        alpha = jnp.exp(m_old - m_new)
        pr = jnp.exp(s - m_new)
        l_ref[h] = alpha * l_ref[h] + jnp.sum(pr, axis=1, keepdims=True)
        vh = vc_ref[pl.ds(h, page, stride=ATTN_HEADS), :].astype(BF16)
        acc_ref[h] = alpha * acc_ref[h] + _dot(pr.astype(BF16), vh)
        m_ref[h] = m_new

    @pl.when(p == n_pages - 1)
    def _():
        lam = _lambda_value(lam_ref, lam_init)
        kn, vn = kn_ref[...], vn_ref[...]
        for h in range(ATTN_HEADS):
            s = jnp.sum(qm_ref[h] * kn[h:h + 1, :], axis=1, keepdims=True)
            m_old = m_ref[h]
            m_new = jnp.maximum(m_old, s)
            alpha = jnp.exp(m_old - m_new)
            pr = jnp.exp(s - m_new)
            l_new = alpha * l_ref[h] + pr
            o = (alpha * acc_ref[h] + pr * vn[h:h + 1, :]) / l_new
            o_ref[h:h + 1, :] = _head_out(o[0:1], o[1:2], lam, sub_ref[...], lam_init).astype(o_ref.dtype)


def _attn_decode(q, k_new, v_new, cache_k, cache_v, page_table, layer, lam_vec, subln, lam_init):
    batch, n_pages = page_table.shape
    rows = cache_k.shape[2]
    page = rows // ATTN_HEADS
    tok_spec = pl.BlockSpec((None, ATTN_HEADS, LANES), lambda b, p, pt: (b, 0, 0))
    page_spec = pl.BlockSpec((None, None, rows, LANES), lambda b, p, pt: (layer, pt[b, p], 0, 0))
    const2 = lambda b, p, pt: (0, 0)
    grid_spec = pltpu.PrefetchScalarGridSpec(
        num_scalar_prefetch=1,
        grid=(batch, n_pages),
        in_specs=[tok_spec, tok_spec, tok_spec, page_spec, page_spec,
                  pl.BlockSpec((4, ATTN_HEAD_DIM), const2), pl.BlockSpec((1, ATTN_VDIM), const2)],
        out_specs=tok_spec,
        scratch_shapes=[pltpu.VMEM((ATTN_HEADS, 8, LANES), F32), pltpu.VMEM((ATTN_HEADS, 8, 1), F32),
                        pltpu.VMEM((ATTN_HEADS, 8, 1), F32), pltpu.VMEM((ATTN_HEADS, 8, LANES), F32)],
    )
    return pl.pallas_call(
        functools.partial(_attn_decode_kernel, page=page, n_pages=n_pages, lam_init=lam_init),
        grid_spec=grid_spec,
        out_shape=jax.ShapeDtypeStruct((batch, ATTN_HEADS, LANES), F32),
        compiler_params=_cparams(2), name="attn_decode",
    )(page_table, q, k_new, v_new, cache_k, cache_v, lam_vec, subln.reshape(1, ATTN_VDIM))


def _pair_ones():
    r = lax.broadcasted_iota(jnp.int32, (LANES, LANES), 0)
    c = lax.broadcasted_iota(jnp.int32, (LANES, LANES), 1)
    return ((r < RWKV_HEAD) == (c < RWKV_HEAD)).astype(BF16)


def _head_sums(z, ones2):
    return jnp.concatenate(
        [_dot(z[:, c * LANES:(c + 1) * LANES].astype(BF16), ones2) for c in range(D_MODEL // LANES)], axis=1)


def _rwkv_pre_kernel(x_ref, p_ref, g_ref, mix_ref, vec_ref, wr_ref, wk_ref, wv_ref, w1_ref, w2_ref, a1_ref, a2_ref,
                     g1_ref, g2_ref, r_o, lw_o, k_o, v_o, kk_o, b_o, gate_o, bonus_o, *, seq_tiles):
    tm = x_ref.shape[0]
    g = g_ref[...]
    hn = _rms(x_ref[...], g, NORM_EPS)
    if seq_tiles:
        hp = _rms(p_ref[7:8, :], g, NORM_EPS)
        hp = jnp.where(pl.program_id(0) % seq_tiles == 0, 0.0, hp)
        row = lax.broadcasted_iota(jnp.int32, (tm, D_MODEL), 0)
        prev = jnp.where(row == 0, hp, pltpu.roll(hn, 1, 0))
    else:
        prev = p_ref[...]
    xx = prev - hn
    mix = mix_ref[...]
    vec = vec_ref[...]
    w0, a0, k_k, k_a, r_k = (vec[c:c + 1] for c in range(5))

    def mixed(c):
        return (hn + xx * mix[c:c + 1]).astype(BF16)

    r = _dot(mixed(0), wr_ref[...])
    z = w0 + _dot(jnp.tanh(_dot(mixed(1), w1_ref[...])).astype(BF16), w2_ref[...])
    k = _dot(mixed(2), wk_ref[...])
    v = _dot(mixed(3), wv_ref[...])
    a = jax.nn.sigmoid(a0 + _dot(_dot(mixed(4), a1_ref[...]).astype(BF16), a2_ref[...]))
    gate = _dot(jax.nn.sigmoid(_dot(mixed(5), g1_ref[...])).astype(BF16), g2_ref[...])
    softplus = jnp.maximum(-z, 0.0) + jnp.log(1.0 + jnp.exp(-jnp.abs(z)))
    lw = -jnp.exp(-softplus - 0.5)
    ones2 = _pair_ones()
    kk = k * k_k
    kk = kk * lax.rsqrt(jnp.maximum(_head_sums(kk * kk, ones2), 1e-24))
    k = k * (1.0 + (a - 1.0) * k_a)
    r_o[...] = r
    lw_o[...] = lw
    k_o[...] = k
    v_o[...] = v
    kk_o[...] = kk
    b_o[...] = kk * a
    gate_o[...] = gate
    bonus_o[...] = _head_sums(r * k * r_k, ones2) * v


def _rwkv_pre(x, prev, g, mix, vec, mats, seq_len):
    m = x.shape[0]
    if prev is None:
        tm = _row_tile(seq_len, 256)
        seq_tiles = seq_len // tm
        p_arg = x
        p_spec = pl.BlockSpec((8, D_MODEL), lambda i: (jnp.maximum(i * (tm // 8) - 1, 0), 0))
    else:
        tm = _row_tile(m, 256)
        seq_tiles = 0
        p_arg = prev
        p_spec = pl.BlockSpec((tm, D_MODEL), lambda i: (i, 0))
    row_spec = pl.BlockSpec((tm, D_MODEL), lambda i: (i, 0))
    return pl.pallas_call(
        functools.partial(_rwkv_pre_kernel, seq_tiles=seq_tiles),
        grid=(m // tm,),
        in_specs=[row_spec, p_spec, _resident((1, D_MODEL)), _resident(mix.shape), _resident(vec.shape)]
                 + [_resident(w.shape) for w in mats],
        out_specs=[row_spec] * 8,
        out_shape=[jax.ShapeDtypeStruct((m, D_MODEL), F32)] * 8,
        compiler_params=_cparams(1), name="rwkv_pre",
    )(x, p_arg, g.reshape(1, D_MODEL), mix, vec, *mats)


def _wkv_kernel(r_ref, lw_ref, k_ref, v_ref, kk_ref, b_ref, gate_ref, bonus_ref, lnw_ref, lnb_ref,
                y_ref, sout_ref, s_ref, *, n_chunks):
    c_len = WKV_CHUNK
    t = pl.program_id(1)

    @pl.when(t == 0)
    def _():
        s_ref[...] = jnp.zeros(s_ref.shape, F32)

    rr = lax.broadcasted_iota(jnp.int32, (c_len, c_len), 0)
    cc = lax.broadcasted_iota(jnp.int32, (c_len, c_len), 1)
    ltri = (rr >= cc).astype(F32)
    r2 = lax.broadcasted_iota(jnp.int32, (LANES, LANES), 0)
    c2 = lax.broadcasted_iota(jnp.int32, (LANES, LANES), 1)
    strict = r2 > c2
    incl = r2 >= c2
    eye = (r2 == c2).astype(F32)
    ones2 = _pair_ones()
    first = lax.broadcasted_iota(jnp.int32, (c_len, LANES), 1) < RWKV_HEAD

    def stack(x):
        return jnp.concatenate([jnp.where(first, x, 0.0), jnp.where(first, 0.0, x)], axis=0)

    def chunk(c, carry):
        rows = pl.ds(pl.multiple_of(c * c_len, c_len), c_len)
        lw = lw_ref[rows, :]
        cum = jnp.dot(ltri, lw, preferred_element_type=F32, precision=HIGHEST)
        e_in = jnp.exp(cum)
        e_neg = jnp.exp(-cum)
        at_all = -(kk_ref[rows, :] * jnp.exp(cum - lw))
        bt_all = b_ref[rows, :] * e_neg
        kt_all = k_ref[rows, :] * e_neg
        rt_all = r_ref[rows, :] * e_in
        v_all = v_ref[rows, :]
        for p in range(RWKV_HEADS // 2):
            sl = slice(p * LANES, (p + 1) * LANES)
            at, bt, kt, rt, v = at_all[:, sl], bt_all[:, sl], kt_all[:, sl], rt_all[:, sl], v_all[:, sl]
            bk = jnp.concatenate([stack(bt), stack(kt)], axis=0).astype(BF16)
            gram = _dot_nt(jnp.concatenate([stack(at), stack(rt)], axis=0).astype(BF16), bk)
            a_ab = jnp.where(strict, gram[:LANES, :LANES], 0.0)
            a_ak = jnp.where(strict, gram[:LANES, LANES:], 0.0)
            r_b = jnp.where(incl, gram[LANES:, :LANES], 0.0)
            r_k = jnp.where(incl, gram[LANES:, LANES:], 0.0)
            tinv = eye + a_ab
            pw = a_ab
            for _ in range(int(math.log2(c_len)) - 1):
                pw = _dot(pw.astype(BF16), pw.astype(BF16))
                tinv = tinv + _dot(tinv.astype(BF16), pw.astype(BF16))
            s = s_ref[p]
            m1 = _dot_nt(jnp.concatenate([at, rt], axis=0).astype(BF16), s.astype(BF16))
            vs = stack(v)
            x = stack(m1[:c_len]) + _dot(a_ak.astype(BF16), vs.astype(BF16))
            tinv = tinv.astype(BF16)
            a_ab = a_ab.astype(BF16)
            u = _dot(tinv, x.astype(BF16))
            for _ in range(WKV_REFINE_STEPS):
                res = x - u + _dot(a_ab, u.astype(BF16))
                u = u + _dot(tinv, res.astype(BF16))
            uv = jnp.concatenate([u, vs], axis=0).astype(BF16)
            y2 = stack(m1[c_len:]) + _dot(jnp.concatenate([r_b, r_k], axis=1).astype(BF16), uv)
            y = y2[:c_len] + y2[c_len:]
            s_ref[p] = (s + _dot_tn(uv, bk)) * e_in[c_len - 1:c_len, sl]
            mu = _dot(y.astype(BF16), ones2) * (1.0 / RWKV_HEAD)
            d = y - mu
            var = _dot((d * d).astype(BF16), ones2) * (1.0 / RWKV_HEAD)
            yn = d * lax.rsqrt(var + RWKV_GN_EPS) * lnw_ref[:, sl] + lnb_ref[:, sl]
            y_ref[rows, sl] = ((yn + bonus_ref[rows, sl]) * gate_ref[rows, sl]).astype(y_ref.dtype)
        return carry

    lax.fori_loop(0, n_chunks, chunk, 0)

    @pl.when(t == pl.num_programs(1) - 1)
    def _():
        sout_ref[...] = s_ref[...]


def _wkv_prompt(streams, ln_w, ln_b, batch, seq_len):
    tw = _row_tile(seq_len, 256)
    nt = seq_len // tw
    row_spec = pl.BlockSpec((tw, D_MODEL), lambda b, t: (b * nt + t, 0))
    n_pairs = RWKV_HEADS // 2
    return pl.pallas_call(
        functools.partial(_wkv_kernel, n_chunks=tw // WKV_CHUNK),
        grid=(batch, nt),
        in_specs=[row_spec] * 8 + [_resident((1, D_MODEL)), _resident((1, D_MODEL))],
        out_specs=[row_spec, pl.BlockSpec((None, n_pairs, LANES, LANES), lambda b, t: (b, 0, 0, 0))],
        out_shape=[jax.ShapeDtypeStruct((batch * seq_len, D_MODEL), BF16),
                   jax.ShapeDtypeStruct((batch, n_pairs, LANES, LANES), F32)],
        scratch_shapes=[pltpu.VMEM((n_pairs, LANES, LANES), F32)],
        compiler_params=_cparams(2), name="wkv_prompt",
    )(*streams, ln_w.reshape(1, D_MODEL), ln_b.reshape(1, D_MODEL))


def _wkv_step_kernel(s_ref, r_ref, lw_ref, k_ref, v_ref, kk_ref, b_ref, gate_ref, bonus_ref, lnw_ref, lnb_ref,
                     y_ref, sout_ref):
    n = RWKV_HEAD
    s = s_ref[...]
    eye = (lax.broadcasted_iota(jnp.int32, (n, n), 0) == lax.broadcasted_iota(jnp.int32, (n, n), 1)).astype(F32)
    sa = -jnp.sum(s * kk_ref[...], axis=-1, keepdims=True)
    vcol = jnp.sum(eye * v_ref[...], axis=-1, keepdims=True)
    s = s * jnp.exp(lw_ref[...]) + sa * b_ref[...] + vcol * k_ref[...]
    sout_ref[...] = s
    ycol = jnp.sum(s * r_ref[...], axis=-1, keepdims=True)
    y = jnp.sum(eye * ycol, axis=-2, keepdims=True)
    mu = jnp.mean(y, axis=-1, keepdims=True)
    d = y - mu
    var = jnp.mean(d * d, axis=-1, keepdims=True)
    yn = d * lax.rsqrt(var + RWKV_GN_EPS) * lnw_ref[...] + lnb_ref[...]
    y_ref[...] = (yn + bonus_ref[...]) * gate_ref[...]


def _wkv_step(state, streams, ln_w, ln_b):
    batch = state.shape[0]
    nb = 8 if batch % 8 == 0 else batch
    vec = lambda z: z.reshape(batch, RWKV_HEADS, 1, RWKV_HEAD)
    st_spec = pl.BlockSpec((nb, RWKV_HEADS, RWKV_HEAD, RWKV_HEAD), lambda i: (i, 0, 0, 0))
    vec_spec = pl.BlockSpec((nb, RWKV_HEADS, 1, RWKV_HEAD), lambda i: (i, 0, 0, 0))
    ln_spec = pl.BlockSpec((1, RWKV_HEADS, 1, RWKV_HEAD), lambda i: (0, 0, 0, 0))
    y, s_new = pl.pallas_call(
        _wkv_step_kernel,
        grid=(batch // nb,),
        in_specs=[st_spec] + [vec_spec] * 8 + [ln_spec, ln_spec],
        out_specs=[vec_spec, st_spec],
        out_shape=[jax.ShapeDtypeStruct((batch, RWKV_HEADS, 1, RWKV_HEAD), F32),
                   jax.ShapeDtypeStruct(state.shape, F32)],
        compiler_params=_cparams(1), name="wkv_step",
    )(state, *[vec(z) for z in streams], ln_w.reshape(1, RWKV_HEADS, 1, RWKV_HEAD),
      ln_b.reshape(1, RWKV_HEADS, 1, RWKV_HEAD))
    return y.reshape(batch, D_MODEL), s_new


CONV_HALO = 32
CONV_TAP_ROWS = 32


def _conv_glu_kernel(x_ref, g_ref, w_ref, b_ref, u_ref):
    h = _rms(x_ref[...], g_ref[...], NORM_EPS).astype(BF16)
    b = b_ref[...]
    a = _dot(h, w_ref[:, :D_MODEL]) + b[:, :D_MODEL]
    gate = _dot(h, w_ref[:, D_MODEL:]) + b[:, D_MODEL:]
    u_ref[...] = a * jax.nn.sigmoid(gate)


def _conv_glu(x, g, w1, b1):
    m = x.shape[0]
    tm = _row_tile(m, 512)
    row_spec = pl.BlockSpec((tm, D_MODEL), lambda i: (i, 0))
    return pl.pallas_call(
        _conv_glu_kernel,
        grid=(m // tm,),
        in_specs=[row_spec, _resident((1, D_MODEL)), _resident((D_MODEL, 2 * D_MODEL)), _resident((1, 2 * D_MODEL))],
        out_specs=row_spec,
        out_shape=jax.ShapeDtypeStruct((m, D_MODEL), F32),
        compiler_params=_cparams(1), name="conv_glu",
    )(x, g.reshape(1, D_MODEL), w1, b1.reshape(1, 2 * D_MODEL))


def _ln_silu(y, ln_w, ln_b):
    mu = jnp.mean(y, axis=-1, keepdims=True)
    d = y - mu
    var = jnp.mean(d * d, axis=-1, keepdims=True)
    yn = d * lax.rsqrt(var + LN_EPS) * ln_w + ln_b
    return yn * jax.nn.sigmoid(yn)


def _conv_dw_kernel(u_ref, halo_ref, dw_ref, vec_ref, w2_ref, res_ref, o_ref, ext_ref, s_ref, *, seq_tiles):
    tm = u_ref.shape[0]
    ext_ref[0:CONV_HALO, :] = jnp.where(pl.program_id(0) % seq_tiles == 0, 0.0, halo_ref[...])
    ext_ref[CONV_HALO:CONV_HALO + tm, :] = u_ref[...]
    vec = vec_ref[...]
    dw_b, ln_w, ln_b, b2 = (vec[c:c + 1] for c in range(4))
    dw = dw_ref[...]
    for c in range(tm // CONV_TAP_ROWS):
        base = c * CONV_TAP_ROWS + CONV_HALO - CONV_BUF
        acc = dw_b + dw[0:1] * ext_ref[base:base + CONV_TAP_ROWS, :]
        for k in range(1, CONV_WIDTH):
            acc = acc + dw[k:k + 1] * ext_ref[base + k:base + k + CONV_TAP_ROWS, :]
        s_ref[c * CONV_TAP_ROWS:(c + 1) * CONV_TAP_ROWS, :] = _ln_silu(acc, ln_w, ln_b).astype(BF16)
    o_ref[...] = res_ref[...] + _dot(s_ref[...], w2_ref[...]) + b2


def _conv_dw(u, dw, vec, w2, res, seq_len):
    m = u.shape[0]
    tm = _row_tile(seq_len, 256)
    per = tm // CONV_HALO
    row_spec = pl.BlockSpec((tm, D_MODEL), lambda i: (i, 0))
    halo_spec = pl.BlockSpec((CONV_HALO, D_MODEL), lambda i: (jnp.maximum(i * per - 1, 0), 0))
    return pl.pallas_call(
        functools.partial(_conv_dw_kernel, seq_tiles=seq_len // tm),
        grid=(m // tm,),
        in_specs=[row_spec, halo_spec, _resident(dw.shape), _resident(vec.shape), _resident((D_MODEL, D_MODEL)), row_spec],
        out_specs=row_spec,
        out_shape=jax.ShapeDtypeStruct((m, D_MODEL), F32),
        scratch_shapes=[pltpu.VMEM((CONV_HALO + tm, D_MODEL), F32), pltpu.VMEM((tm, D_MODEL), BF16)],
        compiler_params=_cparams(1), name="conv_dw",
    )(u, u, dw, vec, w2, res)


def _conv_step_kernel(buf_ref, u_ref, dw_ref, vec_ref, s_ref):
    vec = vec_ref[...]
    dw = dw_ref[...]
    y = (jnp.sum(buf_ref[...] * dw[0:CONV_BUF], axis=1, keepdims=True)
         + u_ref[...] * dw[CONV_BUF:CONV_WIDTH] + vec[0:1])
    s_ref[...] = _ln_silu(y, vec[1:2], vec[2:3])


def _conv_step(buf, u, dw, vec):
    batch = buf.shape[0]
    nb = 8 if batch % 8 == 0 else batch
    one_spec = pl.BlockSpec((nb, 1, D_MODEL), lambda i: (i, 0, 0))
    return pl.pallas_call(
        _conv_step_kernel,
        grid=(batch // nb,),
        in_specs=[pl.BlockSpec((nb, CONV_BUF, D_MODEL), lambda i: (i, 0, 0)), one_spec,
                  _resident(dw.shape), _resident(vec.shape)],
        out_specs=one_spec,
        out_shape=jax.ShapeDtypeStruct((batch, 1, D_MODEL), F32),
        compiler_params=_cparams(1), name="conv_step",
    )(buf, u.reshape(batch, 1, D_MODEL), dw, vec).reshape(batch, D_MODEL)


def _rows8(*rows):
    pad = [jnp.zeros((D_MODEL,), F32)] * (8 - len(rows))
    return jnp.stack([r.reshape(D_MODEL).astype(F32) for r in rows] + pad)


def kernel(x_prompt, x_sample, cache_k, cache_v, page_table, state_wkv, state_shift, state_conv, norm_mix, norm_ffn, norm_final, attn_w_qkv, attn_w_o, attn_lambda, attn_subln, rwkv_mix, rwkv_w0, rwkv_w1, rwkv_w2, rwkv_a0, rwkv_a1, rwkv_a2, rwkv_g1, rwkv_g2, rwkv_k_k, rwkv_k_a, rwkv_r_k, rwkv_wr, rwkv_wk, rwkv_wv, rwkv_wo, rwkv_ln_w, rwkv_ln_b, conv_w1, conv_b1, conv_dw, conv_dw_b, conv_ln_w, conv_ln_b, conv_w2, conv_b2, ffn_w_up, ffn_w_down):
    bp, seq, d = x_prompt.shape
    bs = x_sample.shape[0]
    n_layers_attn, n_pool, page, heads, vdim = cache_k.shape
    past_len = page_table.shape[1] * page
    xp = x_prompt.reshape(bp * seq, d)
    xs = x_sample.reshape(bs, d)
    ck = cache_k.reshape(n_layers_attn, n_pool, page * heads, vdim)
    cv = cache_v.reshape(n_layers_attn, n_pool, page * heads, vdim)
    tabs_p = _rope_tables(jnp.arange(seq, dtype=jnp.int32))
    tabs_s = _rope_tables(jnp.full((bs,), past_len, jnp.int32))
    no_bias = jnp.zeros((d,), F32)
    kp, vp, ks_, vs_, wkv_p, wkv_s, sh_p, sh_s, cv_p, cv_s = ([] for _ in range(10))
    for i in range(DEPTH):
        j = i // N_MIXERS
        kind = i % N_MIXERS
        if kind == 0:
            lam_init = 0.8 - 0.6 * math.exp(-0.3 * i)
            wqkv = attn_w_qkv[j].astype(BF16)
            wo = attn_w_o[j].astype(BF16)
            q, k, v = _attn_qkv(xp, norm_mix[i], wqkv, tabs_p, seq)
            o = _attn_prompt(q, k, v, attn_lambda[j], attn_subln[j], lam_init, bp, seq)
            xp = _matmul_res(o, wo, no_bias, xp)
            q_s, k_s, v_s = _attn_qkv(xs, norm_mix[i], wqkv, tabs_s, bs)
            tok = lambda z: z.reshape(bs, heads, vdim)
            o_s = _attn_decode(tok(q_s.astype(F32)), tok(k_s), tok(v_s), ck, cv, page_table, j,
                               attn_lambda[j], attn_subln[j], lam_init)
            xs = _matmul_res(o_s.reshape(bs, d), wo, no_bias, xs)
            kp.append(k.reshape(bp, seq, heads, vdim))
            vp.append(v.reshape(bp, seq, heads, vdim))
            ks_.append(k_s.reshape(bs, 1, heads, vdim))
            vs_.append(v_s.reshape(bs, 1, heads, vdim))
        elif kind == 1:
            mats = [w[j].astype(BF16) for w in (rwkv_wr, rwkv_wk, rwkv_wv, rwkv_w1, rwkv_w2, rwkv_a1, rwkv_a2,
                                                rwkv_g1, rwkv_g2)]
            vec = _rows8(rwkv_w0[j], rwkv_a0[j], rwkv_k_k[j], rwkv_k_a[j], rwkv_r_k[j])
            wo = rwkv_wo[j].astype(BF16)
            sh_p.append(_final_norm(xp.reshape(bp, seq, d)[:, -1], norm_mix[i]))
            sh_s.append(_final_norm(xs, norm_mix[i]))
            streams = _rwkv_pre(xp, None, norm_mix[i], rwkv_mix[j], vec, mats, seq)
            y, s_pairs = _wkv_prompt(streams, rwkv_ln_w[j], rwkv_ln_b[j], bp, seq)
            xp = _matmul_res(y, wo, no_bias, xp)
            s6 = s_pairs.reshape(bp, RWKV_HEADS // 2, 2, RWKV_HEAD, 2, RWKV_HEAD)
            wkv_p.append(jnp.stack([s6[:, :, 0, :, 0, :], s6[:, :, 1, :, 1, :]], axis=2)
                         .reshape(bp, RWKV_HEADS, RWKV_HEAD, RWKV_HEAD))
            streams = _rwkv_pre(xs, state_shift[j], norm_mix[i], rwkv_mix[j], vec, mats, 1)
            y_s, s_new = _wkv_step(state_wkv[j], streams, rwkv_ln_w[j], rwkv_ln_b[j])
            xs = _matmul_res(y_s, wo, no_bias, xs)
            wkv_s.append(s_new)
        else:
            w1 = conv_w1[j].astype(BF16)
            w2 = conv_w2[j].astype(BF16)
            dw = jnp.concatenate([conv_dw[j], jnp.zeros((1, d), F32)], axis=0)
            vec = _rows8(conv_dw_b[j], conv_ln_w[j], conv_ln_b[j], conv_b2[j])
            u = _conv_glu(xp, norm_mix[i], w1, conv_b1[j])
            xp = _conv_dw(u, dw, vec, w2, xp, seq)
            cv_p.append(u.reshape(bp, seq, d)[:, seq - CONV_BUF:])
            u_s = _conv_glu(xs, norm_mix[i], w1, conv_b1[j])
            xs = _matmul_res(_conv_step(state_conv[j], u_s, dw, vec), w2, conv_b2[j], xs)
            cv_s.append(jnp.concatenate([state_conv[j][:, 1:], u_s[:, None, :]], axis=1))
        wup = ffn_w_up[i].astype(BF16)
        wdn = ffn_w_down[i].astype(BF16)
        xp = _ffn(xp, norm_ffn[i], wup, wdn)
        xs = _ffn(xs, norm_ffn[i], wup, wdn)
    return (_final_norm(xp, norm_final).reshape(bp, seq, d), _final_norm(xs, norm_final).reshape(bs, 1, d),
            jnp.stack(kp), jnp.stack(vp), jnp.stack(ks_), jnp.stack(vs_), jnp.stack(wkv_p), jnp.stack(wkv_s),
            jnp.stack(sh_p), jnp.stack(sh_s), jnp.stack(cv_p), jnp.stack(cv_s))
```

```python
import functools
import math

import jax
import jax.numpy as jnp
from jax import lax
from jax.experimental import pallas as pl
from jax.experimental.pallas import tpu as pltpu

D_MODEL = 1024
DEPTH = 4
N_MIXERS = 3
ATTN_HEAD_DIM = 64
ATTN_HEADS = 8
ATTN_VDIM = 128
ROPE_THETA = 10000.0
NEG_INF = -1e30
RWKV_HEAD = 64
RWKV_HEADS = 16
RWKV_GN_EPS = 64e-5
CONV_WIDTH = 31
CONV_BUF = CONV_WIDTH - 1
LN_EPS = 1e-5
FFN_HIDDEN = 4 * D_MODEL
NORM_EPS = 1e-6

LANES = 128
VMEM_LIMIT = 56 * 1024 * 1024
WKV_CHUNK = 64
WKV_REFINE_STEPS = 2

F32 = jnp.float32
BF16 = jnp.bfloat16
HIGHEST = lax.Precision.HIGHEST


def _cparams(n_axes):
    return pltpu.CompilerParams(dimension_semantics=("arbitrary",) * n_axes, vmem_limit_bytes=VMEM_LIMIT)


def _resident(shape):
    nd = len(shape)
    return pl.BlockSpec(shape, lambda *_: (0,) * nd, pipeline_mode=pl.Buffered(1))


def _dot(a, b):
    return jnp.dot(a, b, preferred_element_type=F32)


def _dot_nt(a, b, precision=None):
    return lax.dot_general(a, b, (((1,), (1,)), ((), ())), preferred_element_type=F32, precision=precision)


def _dot_tn(a, b, precision=None):
    return lax.dot_general(a, b, (((0,), (0,)), ((), ())), preferred_element_type=F32, precision=precision)


def _rms(x, g, eps):
    return x * lax.rsqrt(jnp.mean(x * x, axis=-1, keepdims=True) + eps) * g


def _row_tile(m, want):
    return want if m % want == 0 else m


def _ffn_kernel(x_ref, g_ref, wup_ref, wdn_ref, o_ref, *, chunk):
    x = x_ref[...]
    h = _rms(x, g_ref[...], NORM_EPS).astype(BF16)
    acc = x
    for c in range(FFN_HIDDEN // chunk):
        u = _dot(h, wup_ref[:, c * chunk:(c + 1) * chunk])
        u = jnp.maximum(u, 0.0)
        acc = acc + _dot((u * u).astype(BF16), wdn_ref[c * chunk:(c + 1) * chunk, :])
    o_ref[...] = acc


def _ffn(x, g, wup, wdn):
    m = x.shape[0]
    tm = _row_tile(m, 512)
    return pl.pallas_call(
        functools.partial(_ffn_kernel, chunk=1024),
        grid=(m // tm,),
        in_specs=[pl.BlockSpec((tm, D_MODEL), lambda i: (i, 0)), _resident((1, D_MODEL)),
                  _resident((D_MODEL, FFN_HIDDEN)), _resident((FFN_HIDDEN, D_MODEL))],
        out_specs=pl.BlockSpec((tm, D_MODEL), lambda i: (i, 0)),
        out_shape=jax.ShapeDtypeStruct((m, D_MODEL), F32),
        compiler_params=_cparams(1), name="ffn",
    )(x, g.reshape(1, D_MODEL), wup, wdn)


def _matmul_res_kernel(a_ref, w_ref, b_ref, r_ref, o_ref):
    o_ref[...] = r_ref[...] + _dot(a_ref[...].astype(BF16), w_ref[...]) + b_ref[...]


def _matmul_res(a, w, bias, res):
    m, k = a.shape
    n = w.shape[1]
    tm = _row_tile(m, 512)
    return pl.pallas_call(
        _matmul_res_kernel,
        grid=(m // tm,),
        in_specs=[pl.BlockSpec((tm, k), lambda i: (i, 0)), _resident((k, n)), _resident((1, n)),
                  pl.BlockSpec((tm, n), lambda i: (i, 0))],
        out_specs=pl.BlockSpec((tm, n), lambda i: (i, 0)),
        out_shape=jax.ShapeDtypeStruct((m, n), F32),
        compiler_params=_cparams(1), name="matmul_res",
    )(a, w, bias.reshape(1, n), res)


def _norm_kernel(x_ref, g_ref, o_ref):
    o_ref[...] = _rms(x_ref[...], g_ref[...], NORM_EPS)


def _final_norm(x, g):
    m = x.shape[0]
    tm = _row_tile(m, 1024)
    return pl.pallas_call(
        _norm_kernel,
        grid=(m // tm,),
        in_specs=[pl.BlockSpec((tm, D_MODEL), lambda i: (i, 0)), _resident((1, D_MODEL))],
        out_specs=pl.BlockSpec((tm, D_MODEL), lambda i: (i, 0)),
        out_shape=jax.ShapeDtypeStruct((m, D_MODEL), F32),
        compiler_params=_cparams(1), name="final_norm",
    )(x, g.reshape(1, D_MODEL))


def _rope_tables(pos):
    half = ATTN_HEAD_DIM // 2
    inv = ROPE_THETA ** (-jnp.arange(half, dtype=F32) / half)
    ang = pos.astype(F32)[:, None] * inv[None, :]
    cos, sin = jnp.cos(ang), jnp.sin(ang)
    zero = jnp.zeros_like(sin)
    cos_t = jnp.tile(cos, (1, 4))
    sin_lo = jnp.tile(jnp.concatenate([-sin, zero], axis=1), (1, 2))
    sin_hi = jnp.tile(jnp.concatenate([zero, sin], axis=1), (1, 2))
    return cos_t, sin_lo, sin_hi


def _qkv_kernel(x_ref, g_ref, w_ref, cos_ref, slo_ref, shi_ref, q_ref, k_ref, v_ref, kb_ref, vb_ref):
    h = _rms(x_ref[...], g_ref[...], NORM_EPS).astype(BF16)
    cos, slo, shi = cos_ref[...], slo_ref[...], shi_ref[...]

    def rope(z):
        return z * cos + pltpu.roll(z, LANES - 32, 1) * slo + pltpu.roll(z, 32, 1) * shi

    q = _dot(h, w_ref[:, 0:D_MODEL])
    for c in range(ATTN_HEADS):
        sl = slice(c * LANES, (c + 1) * LANES)
        q_ref[:, sl] = (rope(q[:, sl]) * (ATTN_HEAD_DIM ** -0.5)).astype(BF16)
    k = _dot(h, w_ref[:, D_MODEL:2 * D_MODEL])
    for c in range(ATTN_HEADS):
        sl = slice(c * LANES, (c + 1) * LANES)
        kr = rope(k[:, sl])
        k_ref[:, sl] = kr
        kb_ref[:, sl] = kr.astype(BF16)
    v = _dot(h, w_ref[:, 2 * D_MODEL:3 * D_MODEL])
    v_ref[...] = v
    vb_ref[...] = v.astype(BF16)


def _qkv_kernel_into(x_ref, g_ref, w_ref, cos_ref, slo_ref, shi_ref, k_all_ref, v_all_ref, *out_refs):
    del k_all_ref, v_all_ref
    _qkv_kernel(x_ref, g_ref, w_ref, cos_ref, slo_ref, shi_ref, *out_refs)


def _attn_qkv(x, g, wqkv, tables, seq_len, layer, kv_all):
    m = x.shape[0]
    tm = _row_tile(min(m, seq_len), 512)
    nt = seq_len // tm
    tab_spec = pl.BlockSpec((tm, LANES), lambda i: (i % nt, 0))
    row_spec = pl.BlockSpec((tm, D_MODEL), lambda i: (i, 0))
    slab_spec = pl.BlockSpec((None, tm, D_MODEL), lambda i: (layer, i, 0))
    in_specs = [row_spec, _resident((1, D_MODEL)), _resident((D_MODEL, 3 * D_MODEL)), tab_spec, tab_spec, tab_spec]
    args = [x, g.reshape(1, D_MODEL), wqkv, *tables]
    if isinstance(kv_all, int):
        n_layers, body, aliases = kv_all, _qkv_kernel, {}
    else:
        n_layers, body, aliases = kv_all[0].shape[0], _qkv_kernel_into, {len(args): 1, len(args) + 1: 2}
        in_specs += [pl.BlockSpec(memory_space=pl.ANY)] * 2
        args += list(kv_all)
    stacked = jax.ShapeDtypeStruct((n_layers, m, D_MODEL), F32)
    q, k_all, v_all, kb, vb = pl.pallas_call(
        body,
        grid=(m // tm,),
        in_specs=in_specs,
        out_specs=[row_spec, slab_spec, slab_spec, row_spec, row_spec],
        out_shape=[jax.ShapeDtypeStruct((m, D_MODEL), BF16), stacked, stacked,
                   jax.ShapeDtypeStruct((m, D_MODEL), BF16), jax.ShapeDtypeStruct((m, D_MODEL), BF16)],
        input_output_aliases=aliases,
        compiler_params=_cparams(1), name="attn_qkv",
    )(*args)
    return q, kb, vb, (k_all, v_all)


def _lambda_value(lam_ref, lam_init):
    lv = lam_ref[...]
    a = jnp.sum(lv[0:1] * lv[1:2], axis=1, keepdims=True)
    b = jnp.sum(lv[2:3] * lv[3:4], axis=1, keepdims=True)
    return jnp.exp(a) - jnp.exp(b) + lam_init


def _head_out(o1, o2, lam, subln, lam_init):
    o = o1 - lam * o2
    o = o * lax.rsqrt(jnp.mean(o * o, axis=-1, keepdims=True) + LN_EPS)
    return o * subln * (1.0 - lam_init)


def _attn_prompt_kernel(q_ref, kb_ref, vb_ref, lam_ref, sub_ref, o_ref, qq_ref, m_ref, l_ref, acc_ref,
                        *, tq, lam_init):
    i = pl.program_id(1)
    first = lax.broadcasted_iota(jnp.int32, (tq, LANES), 1) < ATTN_HEAD_DIM
    for h in range(ATTN_HEADS):
        q = q_ref[:, h * LANES:(h + 1) * LANES]
        zero = jnp.zeros_like(q)
        qq_ref[h] = jnp.concatenate([jnp.where(first, q, zero), jnp.where(first, zero, q)], axis=0)
    m_ref[...] = jnp.full(m_ref.shape, NEG_INF, F32)
    l_ref[...] = jnp.zeros(l_ref.shape, F32)
    acc_ref[...] = jnp.zeros(acc_ref.shape, F32)

    def block(j, masked):
        rows = pl.ds(pl.multiple_of(j * tq, tq), tq)
        if masked:
            r = lax.broadcasted_iota(jnp.int32, (2 * tq, tq), 0)
            c = lax.broadcasted_iota(jnp.int32, (2 * tq, tq), 1)
            keep = c <= jnp.where(r >= tq, r - tq, r)
        ones = jnp.ones((tq, LANES), BF16)
        for h in range(ATTN_HEADS):
            sl = slice(h * LANES, (h + 1) * LANES)
            s = _dot_nt(qq_ref[h], kb_ref[rows, sl])
            if masked:
                s = jnp.where(keep, s, NEG_INF)
            m_old = m_ref[h]
            m_new = jnp.maximum(m_old, jnp.max(s, axis=1, keepdims=True))
            alpha = jnp.exp(m_old - m_new)
            p = jnp.exp(s - jnp.concatenate([m_new] * (tq // LANES), axis=1))
            pv = _dot(p.astype(BF16), jnp.concatenate([vb_ref[rows, sl], ones], axis=1))
            l_ref[h] = alpha * l_ref[h] + pv[:, LANES:]
            acc_ref[h] = alpha * acc_ref[h] + pv[:, :LANES]
            m_ref[h] = m_new

    def body(j, carry):
        block(j, False)
        return carry

    lax.fori_loop(0, i, body, 0)
    block(i, True)

    lam = _lambda_value(lam_ref, lam_init)
    for h in range(ATTN_HEADS):
        o = acc_ref[h] / l_ref[h]
        o_ref[:, h * LANES:(h + 1) * LANES] = _head_out(o[:tq], o[tq:], lam, sub_ref[...], lam_init).astype(BF16)


def _attn_prompt(q, kb, vb, lam_vec, subln, lam_init, batch, seq_len):
    tq = 256 if seq_len % 256 == 0 else seq_len
    nq = seq_len // tq
    q_spec = pl.BlockSpec((tq, D_MODEL), lambda b, i: (b * nq + i, 0))
    kv_spec = pl.BlockSpec((seq_len, D_MODEL), lambda b, i: (b, 0))
    return pl.pallas_call(
        functools.partial(_attn_prompt_kernel, tq=tq, lam_init=lam_init),
        grid=(batch, nq),
        in_specs=[q_spec, kv_spec, kv_spec, _resident((4, ATTN_HEAD_DIM)), _resident((1, ATTN_VDIM))],
        out_specs=q_spec,
        out_shape=jax.ShapeDtypeStruct((batch * seq_len, D_MODEL), BF16),
        scratch_shapes=[pltpu.VMEM((ATTN_HEADS, 2 * tq, LANES), BF16), pltpu.VMEM((ATTN_HEADS, 2 * tq, LANES), F32),
                        pltpu.VMEM((ATTN_HEADS, 2 * tq, LANES), F32), pltpu.VMEM((ATTN_HEADS, 2 * tq, LANES), F32)],
        compiler_params=_cparams(2), name="attn_prompt",
    )(q, kb, vb, lam_vec, subln.reshape(1, ATTN_VDIM))


DECODE_PAGES_PER_STEP = 4


def _attn_decode_kernel(pt_ref, q_ref, kn_ref, vn_ref, *refs, pages_per_step, lam_init):
    del pt_ref
    kc_refs, vc_refs = refs[:pages_per_step], refs[pages_per_step:2 * pages_per_step]
    lam_ref, sub_ref, o_ref, m_ref, l_ref, acc_ref = refs[2 * pages_per_step:]
    step = pl.program_id(1)

    @pl.when(step == 0)
    def _():
        m_ref[...] = jnp.full(m_ref.shape, NEG_INF, F32)
        l_ref[...] = jnp.zeros(l_ref.shape, F32)
        acc_ref[...] = jnp.zeros(acc_ref.shape, F32)

    q = q_ref[...]
    first = lax.broadcasted_iota(jnp.int32, (ATTN_HEADS, LANES), 1) < ATTN_HEAD_DIM
    qsel = jnp.concatenate([jnp.where(first, q, 0.0), jnp.where(first, 0.0, q)], axis=0)
    qb = qsel.astype(BF16)
    rows = kc_refs[0].shape[0] * ATTN_HEADS
    same_head = ((lax.broadcasted_iota(jnp.int32, (2 * ATTN_HEADS, rows), 1) & (ATTN_HEADS - 1))
                 == (lax.broadcasted_iota(jnp.int32, (2 * ATTN_HEADS, rows), 0) & (ATTN_HEADS - 1)))
    s = [jnp.where(same_head, _dot_nt(qb, kc_ref[...].reshape(rows, LANES).astype(BF16)), NEG_INF)
         for kc_ref in kc_refs]
    m_old = m_ref[...]
    m_new = m_old
    for si in s:
        m_new = jnp.maximum(m_new, jnp.max(si, axis=1, keepdims=True))
    alpha = jnp.exp(m_old - m_new)
    p = [jnp.exp(si - m_new) for si in s]
    l_new = alpha * l_ref[...]
    acc = alpha * acc_ref[...]
    for pi, vc_ref in zip(p, vc_refs):
        l_new = l_new + jnp.sum(pi, axis=1, keepdims=True)
        acc = acc + _dot(pi.astype(BF16), vc_ref[...].reshape(rows, LANES).astype(BF16))
    m_ref[...] = m_new
    l_ref[...] = l_new
    acc_ref[...] = acc

    @pl.when(step == pl.num_programs(1) - 1)
    def _():
        kn = jnp.concatenate([kn_ref[...]] * 2, axis=0)
        vn = jnp.concatenate([vn_ref[...]] * 2, axis=0)
        sn = jnp.sum(qsel * kn, axis=1, keepdims=True)
        m_fin = jnp.maximum(m_new, sn)
        a_fin = jnp.exp(m_new - m_fin)
        pn = jnp.exp(sn - m_fin)
        o = (a_fin * acc + pn * vn) / (a_fin * l_new + pn)
        lam = _lambda_value(lam_ref, lam_init)
        o_ref[...] = _head_out(o[:ATTN_HEADS], o[ATTN_HEADS:], lam, sub_ref[...], lam_init)


def _attn_decode(q, k_new, v_new, cache_k, cache_v, page_table, layer, lam_vec, subln, lam_init):
    batch, n_pages = page_table.shape
    page = cache_k.shape[2]
    pps = DECODE_PAGES_PER_STEP if n_pages % DECODE_PAGES_PER_STEP == 0 else 1
    tok_spec = pl.BlockSpec((None, ATTN_HEADS, LANES), lambda b, s, pt: (b, 0, 0))
    page_specs = [pl.BlockSpec((None, None, page, ATTN_HEADS, LANES),
                               lambda b, s, pt, i=i: (layer, pt[b, s * pps + i], 0, 0, 0)) for i in range(pps)]
    const2 = lambda b, s, pt: (0, 0)
    grid_spec = pltpu.PrefetchScalarGridSpec(
        num_scalar_prefetch=1,
        grid=(batch, n_pages // pps),
        in_specs=[tok_spec, tok_spec, tok_spec] + page_specs + page_specs
                 + [pl.BlockSpec((4, ATTN_HEAD_DIM), const2), pl.BlockSpec((1, ATTN_VDIM), const2)],
        out_specs=tok_spec,
        scratch_shapes=[pltpu.VMEM((2 * ATTN_HEADS, 1), F32), pltpu.VMEM((2 * ATTN_HEADS, 1), F32),
                        pltpu.VMEM((2 * ATTN_HEADS, LANES), F32)],
    )
    return pl.pallas_call(
        functools.partial(_attn_decode_kernel, pages_per_step=pps, lam_init=lam_init),
        grid_spec=grid_spec,
        out_shape=jax.ShapeDtypeStruct((batch, ATTN_HEADS, LANES), F32),
        compiler_params=_cparams(2), name="attn_decode",
    )(page_table, q, k_new, v_new, *([cache_k] * pps), *([cache_v] * pps), lam_vec, subln.reshape(1, ATTN_VDIM))


def _pair_ones():
    r = lax.broadcasted_iota(jnp.int32, (LANES, LANES), 0)
    c = lax.broadcasted_iota(jnp.int32, (LANES, LANES), 1)
    return ((r < RWKV_HEAD) == (c < RWKV_HEAD)).astype(BF16)


def _head_sums(z, ones2):
    return jnp.concatenate(
        [_dot(z[:, c * LANES:(c + 1) * LANES].astype(BF16), ones2) for c in range(D_MODEL // LANES)], axis=1)


def _rwkv_pre_kernel(x_ref, p_ref, g_ref, mix_ref, vec_ref, wr_ref, wk_ref, wv_ref, w1_ref, w2_ref, a1_ref, a2_ref,
                     g1_ref, g2_ref, r_o, lw_o, k_o, v_o, kk_o, b_o, gate_o, bonus_o, *, seq_tiles):
    tm = x_ref.shape[0]
    g = g_ref[...]
    hn = _rms(x_ref[...], g, NORM_EPS)
    if seq_tiles:
        hp = _rms(p_ref[7:8, :], g, NORM_EPS)
        hp = jnp.where(pl.program_id(0) % seq_tiles == 0, 0.0, hp)
        row = lax.broadcasted_iota(jnp.int32, (tm, D_MODEL), 0)
        prev = jnp.where(row == 0, hp, pltpu.roll(hn, 1, 0))
    else:
        prev = p_ref[...]
    xx = prev - hn
    mix = mix_ref[...]
    vec = vec_ref[...]
    w0, a0, k_k, k_a, r_k = (vec[c:c + 1] for c in range(5))

    def mixed(c):
        return (hn + xx * mix[c:c + 1]).astype(BF16)

    r = _dot(mixed(0), wr_ref[...])
    z = w0 + _dot(jnp.tanh(_dot(mixed(1), w1_ref[...])).astype(BF16), w2_ref[...])
    k = _dot(mixed(2), wk_ref[...])
    v = _dot(mixed(3), wv_ref[...])
    a = jax.nn.sigmoid(a0 + _dot(_dot(mixed(4), a1_ref[...]).astype(BF16), a2_ref[...]))
    gate = _dot(jax.nn.sigmoid(_dot(mixed(5), g1_ref[...])).astype(BF16), g2_ref[...])
    softplus = jnp.maximum(-z, 0.0) + jnp.log(1.0 + jnp.exp(-jnp.abs(z)))
    lw = -jnp.exp(-softplus - 0.5)
    ones2 = _pair_ones()
    kk = k * k_k
    kk = kk * lax.rsqrt(jnp.maximum(_head_sums(kk * kk, ones2), 1e-24))
    k = k * (1.0 + (a - 1.0) * k_a)
    r_o[...] = r
    lw_o[...] = lw
    k_o[...] = k
    v_o[...] = v
    kk_o[...] = kk
    b_o[...] = kk * a
    gate_o[...] = gate
    bonus_o[...] = _head_sums(r * k * r_k, ones2) * v


def _rwkv_pre(x, prev, g, mix, vec, mats, seq_len):
    m = x.shape[0]
    if prev is None:
        tm = _row_tile(seq_len, 256)
        seq_tiles = seq_len // tm
        p_arg = x
        p_spec = pl.BlockSpec((8, D_MODEL), lambda i: (jnp.maximum(i * (tm // 8) - 1, 0), 0))
    else:
        tm = _row_tile(m, 256)
        seq_tiles = 0
        p_arg = prev
        p_spec = pl.BlockSpec((tm, D_MODEL), lambda i: (i, 0))
    row_spec = pl.BlockSpec((tm, D_MODEL), lambda i: (i, 0))
    return pl.pallas_call(
        functools.partial(_rwkv_pre_kernel, seq_tiles=seq_tiles),
        grid=(m // tm,),
        in_specs=[row_spec, p_spec, _resident((1, D_MODEL)), _resident(mix.shape), _resident(vec.shape)]
                 + [_resident(w.shape) for w in mats],
        out_specs=[row_spec] * 8,
        out_shape=[jax.ShapeDtypeStruct((m, D_MODEL), F32)] * 8,
        compiler_params=_cparams(1), name="rwkv_pre",
    )(x, p_arg, g.reshape(1, D_MODEL), mix, vec, *mats)


def _wkv_kernel(r_ref, lw_ref, k_ref, v_ref, kk_ref, b_ref, gate_ref, bonus_ref, lnw_ref, lnb_ref,
                y_ref, sout_ref, s_ref, ar_ref, bk_ref, tinv_ref, aab_ref, rb_ref, x0_ref, y0_ref, pc_ref,
                *, n_chunks):
    c_len = WKV_CHUNK
    n_pairs = RWKV_HEADS // 2
    pairs = range(n_pairs)
    t = pl.program_id(1)

    @pl.when(t == 0)
    def _():
        s_ref[...] = jnp.zeros(s_ref.shape, F32)

    rr = lax.broadcasted_iota(jnp.int32, (c_len, c_len), 0)
    cc = lax.broadcasted_iota(jnp.int32, (c_len, c_len), 1)
    ltri = (rr >= cc).astype(F32)
    r2 = lax.broadcasted_iota(jnp.int32, (LANES, LANES), 0)
    c2 = lax.broadcasted_iota(jnp.int32, (LANES, LANES), 1)
    strict = r2 > c2
    incl = r2 >= c2
    eye = (r2 == c2).astype(F32)
    ones2 = _pair_ones()
    first = lax.broadcasted_iota(jnp.int32, (c_len, LANES), 1) < RWKV_HEAD
    lanes = [slice(p * LANES, (p + 1) * LANES) for p in pairs]

    def stack(x):
        zero = jnp.zeros_like(x)
        return jnp.concatenate([jnp.where(first, x, zero), jnp.where(first, zero, x)], axis=0)

    def prepare(c, carry):
        rows = pl.ds(pl.multiple_of(c * c_len, c_len), c_len)
        lw = lw_ref[rows, :]
        cum = jnp.dot(ltri, lw, preferred_element_type=F32, precision=HIGHEST)
        e_in = jnp.exp(cum)
        e_neg = jnp.exp(-cum)
        at_all = (-(kk_ref[rows, :] * jnp.exp(cum - lw))).astype(BF16)
        rt_all = (r_ref[rows, :] * e_in).astype(BF16)
        bt_all = (b_ref[rows, :] * e_neg).astype(BF16)
        kt_all = (k_ref[rows, :] * e_neg).astype(BF16)
        v_all = v_ref[rows, :].astype(BF16)
        ar_ref[c] = jnp.concatenate([at_all, rt_all], axis=0)
        pc_ref[c] = e_in[c_len - 8:c_len, :]
        bk = [jnp.concatenate([stack(bt_all[:, sl]), stack(kt_all[:, sl])], axis=0) for sl in lanes]
        ar = [jnp.concatenate([stack(at_all[:, sl]), stack(rt_all[:, sl])], axis=0) for sl in lanes]
        gram = [_dot_nt(ar[p], bk[p]) for p in pairs]
        vs = [stack(v_all[:, sl]) for sl in lanes]
        a_ab = [jnp.where(strict, gram[p][:LANES, :LANES], 0.0) for p in pairs]
        a_ak = [jnp.where(strict, gram[p][:LANES, LANES:], 0.0).astype(BF16) for p in pairs]
        r_k = [jnp.where(incl, gram[p][LANES:, LANES:], 0.0).astype(BF16) for p in pairs]
        for p in pairs:
            bk_ref[c, p] = bk[p]
            rb_ref[c, p] = jnp.where(incl, gram[p][LANES:, :LANES], 0.0).astype(BF16)
        x0 = [_dot(a_ak[p], vs[p]) for p in pairs]
        y0 = [_dot(r_k[p], vs[p]) for p in pairs]
        for p in pairs:
            x0_ref[c, p] = x0[p]
            y0_ref[c, p] = y0[p]
        tinv = [eye + a_ab[p] for p in pairs]
        pw = [a_ab[p].astype(BF16) for p in pairs]
        for p in pairs:
            aab_ref[c, p] = pw[p]
        for _ in range(int(math.log2(c_len)) - 1):
            pw = [_dot(pw[p], pw[p]).astype(BF16) for p in pairs]
            tinv = [tinv[p] + _dot(tinv[p].astype(BF16), pw[p]) for p in pairs]
        for p in pairs:
            tinv_ref[c, p] = tinv[p].astype(BF16)
        return carry

    lax.fori_loop(0, n_chunks, prepare, 0)

    def advance(c, carry):
        rows = pl.ds(pl.multiple_of(c * c_len, c_len), c_len)
        ar = ar_ref[c]
        s = [s_ref[p] for p in pairs]
        m1 = [_dot_nt(ar[:, lanes[p]], s[p].astype(BF16)) for p in pairs]
        x = [stack(m1[p][:c_len]) + x0_ref[c, p] for p in pairs]
        u = [_dot(tinv_ref[c, p], x[p].astype(BF16)) for p in pairs]
        for _ in range(WKV_REFINE_STEPS):
            res = [x[p] - u[p] + _dot(aab_ref[c, p], u[p].astype(BF16)) for p in pairs]
            u = [u[p] + _dot(tinv_ref[c, p], res[p].astype(BF16)) for p in pairs]
        ub = [u[p].astype(BF16) for p in pairs]
        vs = [stack(v_ref[rows, lanes[p]].astype(BF16)) for p in pairs]
        y2 = [stack(m1[p][c_len:]) + y0_ref[c, p] + _dot(rb_ref[c, p], ub[p]) for p in pairs]
        upd = [_dot_tn(jnp.concatenate([ub[p], vs[p]], axis=0), bk_ref[c, p]) for p in pairs]
        decay = pc_ref[c]
        for p in pairs:
            s_ref[p] = (s[p] + upd[p]) * decay[7:8, lanes[p]]
        y = [y2[p][:c_len] + y2[p][c_len:] for p in pairs]
        mu = [_dot(y[p].astype(BF16), ones2) * (1.0 / RWKV_HEAD) for p in pairs]
        d = [y[p] - mu[p] for p in pairs]
        var = [_dot((d[p] * d[p]).astype(BF16), ones2) * (1.0 / RWKV_HEAD) for p in pairs]
        for p in pairs:
            sl = lanes[p]
            yn = d[p] * lax.rsqrt(var[p] + RWKV_GN_EPS) * lnw_ref[:, sl] + lnb_ref[:, sl]
            y_ref[rows, sl] = ((yn + bonus_ref[rows, sl]) * gate_ref[rows, sl]).astype(y_ref.dtype)
        return carry

    lax.fori_loop(0, n_chunks, advance, 0)

    @pl.when(t == pl.num_programs(1) - 1)
    def _():
        sout_ref[...] = s_ref[...]


def _wkv_prompt(streams, ln_w, ln_b, batch, seq_len):
    tw = _row_tile(seq_len, 256)
    nt = seq_len // tw
    nc = tw // WKV_CHUNK
    row_spec = pl.BlockSpec((tw, D_MODEL), lambda b, t: (b * nt + t, 0))
    n_pairs = RWKV_HEADS // 2
    pair_mat = lambda rows, cols, dt: pltpu.VMEM((nc, n_pairs, rows, cols), dt)
    return pl.pallas_call(
        functools.partial(_wkv_kernel, n_chunks=nc),
        grid=(batch, nt),
        in_specs=[row_spec] * 8 + [_resident((1, D_MODEL)), _resident((1, D_MODEL))],
        out_specs=[row_spec, pl.BlockSpec((None, n_pairs, LANES, LANES), lambda b, t: (b, 0, 0, 0))],
        out_shape=[jax.ShapeDtypeStruct((batch * seq_len, D_MODEL), BF16),
                   jax.ShapeDtypeStruct((batch, n_pairs, LANES, LANES), F32)],
        scratch_shapes=[pltpu.VMEM((n_pairs, LANES, LANES), F32),
                        pltpu.VMEM((nc, 2 * WKV_CHUNK, D_MODEL), BF16),
                        pair_mat(4 * WKV_CHUNK, LANES, BF16),
                        pair_mat(LANES, LANES, BF16), pair_mat(LANES, LANES, BF16), pair_mat(LANES, LANES, BF16),
                        pair_mat(LANES, LANES, F32), pair_mat(LANES, LANES, F32),
                        pltpu.VMEM((nc, 8, D_MODEL), F32)],
        compiler_params=_cparams(2), name="wkv_prompt",
    )(*streams, ln_w.reshape(1, D_MODEL), ln_b.reshape(1, D_MODEL))


def _wkv_step_kernel(s_ref, r_ref, lw_ref, k_ref, v_ref, kk_ref, b_ref, gate_ref, bonus_ref, lnw_ref, lnb_ref,
                     y_ref, sout_ref):
    n = RWKV_HEAD
    s = s_ref[...]
    eye = (lax.broadcasted_iota(jnp.int32, (n, n), 0) == lax.broadcasted_iota(jnp.int32, (n, n), 1)).astype(F32)
    sa = -jnp.sum(s * kk_ref[...], axis=-1, keepdims=True)
    vcol = jnp.sum(eye * v_ref[...], axis=-1, keepdims=True)
    s = s * jnp.exp(lw_ref[...]) + sa * b_ref[...] + vcol * k_ref[...]
    sout_ref[...] = s
    ycol = jnp.sum(s * r_ref[...], axis=-1, keepdims=True)
    y = jnp.sum(eye * ycol, axis=-2, keepdims=True)
    mu = jnp.mean(y, axis=-1, keepdims=True)
    d = y - mu
    var = jnp.mean(d * d, axis=-1, keepdims=True)
    yn = d * lax.rsqrt(var + RWKV_GN_EPS) * lnw_ref[...] + lnb_ref[...]
    y_ref[...] = (yn + bonus_ref[...]) * gate_ref[...]


def _wkv_step(state, streams, ln_w, ln_b):
    batch = state.shape[0]
    nb = 8 if batch % 8 == 0 else batch
    vec = lambda z: z.reshape(batch, RWKV_HEADS, 1, RWKV_HEAD)
    st_spec = pl.BlockSpec((nb, RWKV_HEADS, RWKV_HEAD, RWKV_HEAD), lambda i: (i, 0, 0, 0))
    vec_spec = pl.BlockSpec((nb, RWKV_HEADS, 1, RWKV_HEAD), lambda i: (i, 0, 0, 0))
    ln_spec = pl.BlockSpec((1, RWKV_HEADS, 1, RWKV_HEAD), lambda i: (0, 0, 0, 0))
    y, s_new = pl.pallas_call(
        _wkv_step_kernel,
        grid=(batch // nb,),
        in_specs=[st_spec] + [vec_spec] * 8 + [ln_spec, ln_spec],
        out_specs=[vec_spec, st_spec],
        out_shape=[jax.ShapeDtypeStruct((batch, RWKV_HEADS, 1, RWKV_HEAD), F32),
                   jax.ShapeDtypeStruct(state.shape, F32)],
        compiler_params=_cparams(1), name="wkv_step",
    )(state, *[vec(z) for z in streams], ln_w.reshape(1, RWKV_HEADS, 1, RWKV_HEAD),
      ln_b.reshape(1, RWKV_HEADS, 1, RWKV_HEAD))
    return y.reshape(batch, D_MODEL), s_new


CONV_HALO = 32
CONV_TAP_ROWS = 64


def _conv_glu_kernel(x_ref, g_ref, w_ref, b_ref, u_ref):
    h = _rms(x_ref[...], g_ref[...], NORM_EPS).astype(BF16)
    b = b_ref[...]
    a = _dot(h, w_ref[:, :D_MODEL]) + b[:, :D_MODEL]
    gate = _dot(h, w_ref[:, D_MODEL:]) + b[:, D_MODEL:]
    u_ref[...] = a * jax.nn.sigmoid(gate)


def _conv_glu(x, g, w1, b1):
    m = x.shape[0]
    tm = _row_tile(m, 512)
    row_spec = pl.BlockSpec((tm, D_MODEL), lambda i: (i, 0))
    return pl.pallas_call(
        _conv_glu_kernel,
        grid=(m // tm,),
        in_specs=[row_spec, _resident((1, D_MODEL)), _resident((D_MODEL, 2 * D_MODEL)), _resident((1, 2 * D_MODEL))],
        out_specs=row_spec,
        out_shape=jax.ShapeDtypeStruct((m, D_MODEL), F32),
        compiler_params=_cparams(1), name="conv_glu",
    )(x, g.reshape(1, D_MODEL), w1, b1.reshape(1, 2 * D_MODEL))


def _ln_silu(y, ln_w, ln_b):
    mu = jnp.mean(y, axis=-1, keepdims=True)
    d = y - mu
    var = jnp.mean(d * d, axis=-1, keepdims=True)
    yn = d * lax.rsqrt(var + LN_EPS) * ln_w + ln_b
    return yn * jax.nn.sigmoid(yn)


def _conv_dw_kernel(u_ref, halo_ref, dw_ref, vec_ref, w2_ref, res_ref, o_ref, ext_ref, s_ref, *, seq_tiles):
    tm = u_ref.shape[0]
    ext_ref[0:CONV_HALO, :] = jnp.where(pl.program_id(0) % seq_tiles == 0, 0.0, halo_ref[...])
    ext_ref[CONV_HALO:CONV_HALO + tm, :] = u_ref[...]
    vec = vec_ref[...]
    dw_b, ln_w, ln_b, b2 = (vec[c:c + 1] for c in range(4))
    dw = dw_ref[...]
    phases = [[] for _ in range(8)]
    for k in range(CONV_WIDTH):
        off = CONV_HALO - CONV_BUF + k
        phases[off % 8].append((k, off - off % 8))
    for c in range(tm // CONV_TAP_ROWS):
        base = c * CONV_TAP_ROWS
        acc = dw_b
        for b, taps in enumerate(phases):
            part = None
            n_rows = CONV_TAP_ROWS + (8 if b else 0)
            for k, off in taps:
                term = dw[k:k + 1] * ext_ref[base + off:base + off + n_rows, :]
                part = term if part is None else part + term
            acc = acc + part[b:b + CONV_TAP_ROWS]
        s_ref[base:base + CONV_TAP_ROWS, :] = _ln_silu(acc, ln_w, ln_b).astype(BF16)
    o_ref[...] = res_ref[...] + _dot(s_ref[...], w2_ref[...]) + b2


def _conv_dw(u, dw, vec, w2, res, seq_len):
    m = u.shape[0]
    tm = _row_tile(seq_len, 256)
    per = tm // CONV_HALO
    row_spec = pl.BlockSpec((tm, D_MODEL), lambda i: (i, 0))
    halo_spec = pl.BlockSpec((CONV_HALO, D_MODEL), lambda i: (jnp.maximum(i * per - 1, 0), 0))
    return pl.pallas_call(
        functools.partial(_conv_dw_kernel, seq_tiles=seq_len // tm),
        grid=(m // tm,),
        in_specs=[row_spec, halo_spec, _resident(dw.shape), _resident(vec.shape), _resident((D_MODEL, D_MODEL)), row_spec],
        out_specs=row_spec,
        out_shape=jax.ShapeDtypeStruct((m, D_MODEL), F32),
        scratch_shapes=[pltpu.VMEM((CONV_HALO + tm, D_MODEL), F32), pltpu.VMEM((tm, D_MODEL), BF16)],
        compiler_params=_cparams(1), name="conv_dw",
    )(u, u, dw, vec, w2, res)


def _conv_step_kernel(buf_ref, u_ref, dw_ref, vec_ref, s_ref):
    vec = vec_ref[...]
    dw = dw_ref[...]
    y = (jnp.sum(buf_ref[...] * dw[0:CONV_BUF], axis=1, keepdims=True)
         + u_ref[...] * dw[CONV_BUF:CONV_WIDTH] + vec[0:1])
    s_ref[...] = _ln_silu(y, vec[1:2], vec[2:3])


def _conv_step(buf, u, dw, vec):
    batch = buf.shape[0]
    nb = 8 if batch % 8 == 0 else batch
    one_spec = pl.BlockSpec((nb, 1, D_MODEL), lambda i: (i, 0, 0))
    return pl.pallas_call(
        _conv_step_kernel,
        grid=(batch // nb,),
        in_specs=[pl.BlockSpec((nb, CONV_BUF, D_MODEL), lambda i: (i, 0, 0)), one_spec,
                  _resident(dw.shape), _resident(vec.shape)],
        out_specs=one_spec,
        out_shape=jax.ShapeDtypeStruct((batch, 1, D_MODEL), F32),
        compiler_params=_cparams(1), name="conv_step",
    )(buf, u.reshape(batch, 1, D_MODEL), dw, vec).reshape(batch, D_MODEL)


def _rows8(*rows):
    pad = [jnp.zeros((D_MODEL,), F32)] * (8 - len(rows))
    return jnp.stack([r.reshape(D_MODEL).astype(F32) for r in rows] + pad)


def kernel(x_prompt, x_sample, cache_k, cache_v, page_table, state_wkv, state_shift, state_conv, norm_mix, norm_ffn, norm_final, attn_w_qkv, attn_w_o, attn_lambda, attn_subln, rwkv_mix, rwkv_w0, rwkv_w1, rwkv_w2, rwkv_a0, rwkv_a1, rwkv_a2, rwkv_g1, rwkv_g2, rwkv_k_k, rwkv_k_a, rwkv_r_k, rwkv_wr, rwkv_wk, rwkv_wv, rwkv_wo, rwkv_ln_w, rwkv_ln_b, conv_w1, conv_b1, conv_dw, conv_dw_b, conv_ln_w, conv_ln_b, conv_w2, conv_b2, ffn_w_up, ffn_w_down):
    bp, seq, d = x_prompt.shape
    bs = x_sample.shape[0]
    n_layers_attn, n_pool, page, heads, vdim = cache_k.shape
    past_len = page_table.shape[1] * page
    xp = x_prompt.reshape(bp * seq, d)
    xs = x_sample.reshape(bs, d)
    tabs_p = _rope_tables(jnp.arange(seq, dtype=jnp.int32))
    tabs_s = _rope_tables(jnp.full((bs,), past_len, jnp.int32))
    no_bias = jnp.zeros((d,), F32)
    kv_p = kv_s = n_layers_attn
    wkv_p, wkv_s, sh_p, sh_s, cv_p, cv_s = ([] for _ in range(6))
    for i in range(DEPTH):
        j = i // N_MIXERS
        kind = i % N_MIXERS
        if kind == 0:
            lam_init = 0.8 - 0.6 * math.exp(-0.3 * i)
            wqkv = attn_w_qkv[j].astype(BF16)
            wo = attn_w_o[j].astype(BF16)
            q, kb, vb, kv_p = _attn_qkv(xp, norm_mix[i], wqkv, tabs_p, seq, j, kv_p)
            o = _attn_prompt(q, kb, vb, attn_lambda[j], attn_subln[j], lam_init, bp, seq)
            xp = _matmul_res(o, wo, no_bias, xp)
            q_s, _, _, kv_s = _attn_qkv(xs, norm_mix[i], wqkv, tabs_s, bs, j, kv_s)
            tok = lambda z: z.reshape(bs, heads, vdim)
            o_s = _attn_decode(tok(q_s.astype(F32)), tok(kv_s[0][j]), tok(kv_s[1][j]), cache_k, cache_v, page_table,
                               j, attn_lambda[j], attn_subln[j], lam_init)
            xs = _matmul_res(o_s.reshape(bs, d), wo, no_bias, xs)
        elif kind == 1:
            mats = [w[j].astype(BF16) for w in (rwkv_wr, rwkv_wk, rwkv_wv, rwkv_w1, rwkv_w2, rwkv_a1, rwkv_a2,
                                                rwkv_g1, rwkv_g2)]
            vec = _rows8(rwkv_w0[j], rwkv_a0[j], rwkv_k_k[j], rwkv_k_a[j], rwkv_r_k[j])
            wo = rwkv_wo[j].astype(BF16)
            sh_p.append(_final_norm(xp.reshape(bp, seq, d)[:, -1], norm_mix[i]))
            sh_s.append(_final_norm(xs, norm_mix[i]))
            streams = _rwkv_pre(xp, None, norm_mix[i], rwkv_mix[j], vec, mats, seq)
            y, s_pairs = _wkv_prompt(streams, rwkv_ln_w[j], rwkv_ln_b[j], bp, seq)
            xp = _matmul_res(y, wo, no_bias, xp)
            s6 = s_pairs.reshape(bp, RWKV_HEADS // 2, 2, RWKV_HEAD, 2, RWKV_HEAD)
            wkv_p.append(jnp.stack([s6[:, :, 0, :, 0, :], s6[:, :, 1, :, 1, :]], axis=2)
                         .reshape(bp, RWKV_HEADS, RWKV_HEAD, RWKV_HEAD))
            streams = _rwkv_pre(xs, state_shift[j], norm_mix[i], rwkv_mix[j], vec, mats, 1)
            y_s, s_new = _wkv_step(state_wkv[j], streams, rwkv_ln_w[j], rwkv_ln_b[j])
            xs = _matmul_res(y_s, wo, no_bias, xs)
            wkv_s.append(s_new)
        else:
            w1 = conv_w1[j].astype(BF16)
            w2 = conv_w2[j].astype(BF16)
            dw = jnp.concatenate([conv_dw[j], jnp.zeros((1, d), F32)], axis=0)
            vec = _rows8(conv_dw_b[j], conv_ln_w[j], conv_ln_b[j], conv_b2[j])
            u = _conv_glu(xp, norm_mix[i], w1, conv_b1[j])
            xp = _conv_dw(u, dw, vec, w2, xp, seq)
            cv_p.append(u.reshape(bp, seq, d)[:, seq - CONV_BUF:])
            u_s = _conv_glu(xs, norm_mix[i], w1, conv_b1[j])
            xs = _matmul_res(_conv_step(state_conv[j], u_s, dw, vec), w2, conv_b2[j], xs)
            cv_s.append(jnp.concatenate([state_conv[j][:, 1:], u_s[:, None, :]], axis=1))
        wup = ffn_w_up[i].astype(BF16)
        wdn = ffn_w_down[i].astype(BF16)
        xp = _ffn(xp, norm_ffn[i], wup, wdn)
        xs = _ffn(xs, norm_ffn[i], wup, wdn)
    return (_final_norm(xp, norm_final).reshape(bp, seq, d), _final_norm(xs, norm_final).reshape(bs, 1, d),
            kv_p[0].reshape(n_layers_attn, bp, seq, heads, vdim), kv_p[1].reshape(n_layers_attn, bp, seq, heads, vdim),
            kv_s[0].reshape(n_layers_attn, bs, 1, heads, vdim), kv_s[1].reshape(n_layers_attn, bs, 1, heads, vdim),
            jnp.stack(wkv_p), jnp.stack(wkv_s),
            jnp.stack(sh_p), jnp.stack(sh_s), jnp.stack(cv_p), jnp.stack(cv_s))
```

```python
import functools
import math

import jax
import jax.numpy as jnp
from jax import lax
from jax.experimental import pallas as pl
from jax.experimental.pallas import tpu as pltpu

D_MODEL = 1024
DEPTH = 4
N_MIXERS = 3
ATTN_HEAD_DIM = 64
ATTN_HEADS = 8
ATTN_VDIM = 128
ROPE_THETA = 10000.0
NEG_INF = -1e30
RWKV_HEAD = 64
RWKV_HEADS = 16
RWKV_GN_EPS = 64e-5
CONV_WIDTH = 31
CONV_BUF = CONV_WIDTH - 1
LN_EPS = 1e-5
FFN_HIDDEN = 4 * D_MODEL
NORM_EPS = 1e-6

LANES = 128
VMEM_LIMIT = 56 * 1024 * 1024
WKV_CHUNK = 64
WKV_REFINE_STEPS = 1
Q_SCALE = ATTN_HEAD_DIM ** -0.5 * math.log2(math.e)

F32 = jnp.float32
BF16 = jnp.bfloat16
HIGHEST = lax.Precision.HIGHEST


def _cparams(n_axes):
    return pltpu.CompilerParams(dimension_semantics=("arbitrary",) * n_axes, vmem_limit_bytes=VMEM_LIMIT)


def _resident(shape):
    nd = len(shape)
    return pl.BlockSpec(shape, lambda *_: (0,) * nd, pipeline_mode=pl.Buffered(1))


def _dot(a, b):
    return jnp.dot(a, b, preferred_element_type=F32)


def _dot_nt(a, b, precision=None):
    return lax.dot_general(a, b, (((1,), (1,)), ((), ())), preferred_element_type=F32, precision=precision)


def _dot_tn(a, b, precision=None):
    return lax.dot_general(a, b, (((0,), (0,)), ((), ())), preferred_element_type=F32, precision=precision)


def _rms(x, g, eps):
    return x * lax.rsqrt(jnp.mean(x * x, axis=-1, keepdims=True) + eps) * g


def _row_tile(m, want):
    return want if m % want == 0 else m


def _ffn_kernel(x_ref, g_ref, wup_ref, wdn_ref, o_ref, *, chunk):
    x = x_ref[...]
    h = _rms(x, g_ref[...], NORM_EPS).astype(BF16)
    acc = x
    for c in range(FFN_HIDDEN // chunk):
        u = _dot(h, wup_ref[:, c * chunk:(c + 1) * chunk])
        u = jnp.maximum(u, 0.0)
        acc = acc + _dot((u * u).astype(BF16), wdn_ref[c * chunk:(c + 1) * chunk, :])
    o_ref[...] = acc


def _ffn(x, g, wup, wdn):
    m = x.shape[0]
    tm = _row_tile(m, 512)
    return pl.pallas_call(
        functools.partial(_ffn_kernel, chunk=1024),
        grid=(m // tm,),
        in_specs=[pl.BlockSpec((tm, D_MODEL), lambda i: (i, 0)), _resident((1, D_MODEL)),
                  _resident((D_MODEL, FFN_HIDDEN)), _resident((FFN_HIDDEN, D_MODEL))],
        out_specs=pl.BlockSpec((tm, D_MODEL), lambda i: (i, 0)),
        out_shape=jax.ShapeDtypeStruct((m, D_MODEL), F32),
        compiler_params=_cparams(1), name="ffn",
    )(x, g.reshape(1, D_MODEL), wup, wdn)


def _matmul_res_kernel(a_ref, w_ref, b_ref, r_ref, o_ref):
    o_ref[...] = r_ref[...] + _dot(a_ref[...].astype(BF16), w_ref[...]) + b_ref[...]


def _matmul_res(a, w, bias, res):
    m, k = a.shape
    n = w.shape[1]
    tm = _row_tile(m, 512)
    return pl.pallas_call(
        _matmul_res_kernel,
        grid=(m // tm,),
        in_specs=[pl.BlockSpec((tm, k), lambda i: (i, 0)), _resident((k, n)), _resident((1, n)),
                  pl.BlockSpec((tm, n), lambda i: (i, 0))],
        out_specs=pl.BlockSpec((tm, n), lambda i: (i, 0)),
        out_shape=jax.ShapeDtypeStruct((m, n), F32),
        compiler_params=_cparams(1), name="matmul_res",
    )(a, w, bias.reshape(1, n), res)


def _norm_kernel(x_ref, g_ref, o_ref):
    o_ref[...] = _rms(x_ref[...], g_ref[...], NORM_EPS)


def _final_norm(x, g):
    m = x.shape[0]
    tm = _row_tile(m, 1024)
    return pl.pallas_call(
        _norm_kernel,
        grid=(m // tm,),
        in_specs=[pl.BlockSpec((tm, D_MODEL), lambda i: (i, 0)), _resident((1, D_MODEL))],
        out_specs=pl.BlockSpec((tm, D_MODEL), lambda i: (i, 0)),
        out_shape=jax.ShapeDtypeStruct((m, D_MODEL), F32),
        compiler_params=_cparams(1), name="final_norm",
    )(x, g.reshape(1, D_MODEL))


def _rope_tables(pos):
    half = ATTN_HEAD_DIM // 2
    inv = ROPE_THETA ** (-jnp.arange(half, dtype=F32) / half)
    ang = pos.astype(F32)[:, None] * inv[None, :]
    cos, sin = jnp.cos(ang), jnp.sin(ang)
    zero = jnp.zeros_like(sin)
    cos_t = jnp.tile(cos, (1, 4))
    sin_lo = jnp.tile(jnp.concatenate([-sin, zero], axis=1), (1, 2))
    sin_hi = jnp.tile(jnp.concatenate([zero, sin], axis=1), (1, 2))
    return cos_t, sin_lo, sin_hi


def _qkv_kernel(x_ref, g_ref, w_ref, cos_ref, slo_ref, shi_ref, q_ref, k_ref, v_ref, kb_ref, vb_ref):
    h = _rms(x_ref[...], g_ref[...], NORM_EPS).astype(BF16)
    cos, slo, shi = cos_ref[...], slo_ref[...], shi_ref[...]

    def rope(z):
        return z * cos + pltpu.roll(z, LANES - 32, 1) * slo + pltpu.roll(z, 32, 1) * shi

    q = _dot(h, w_ref[:, 0:D_MODEL])
    for c in range(ATTN_HEADS):
        sl = slice(c * LANES, (c + 1) * LANES)
        q_ref[:, sl] = (rope(q[:, sl]) * Q_SCALE).astype(BF16)
    tm = x_ref.shape[0]
    k = _dot(h, w_ref[:, D_MODEL:2 * D_MODEL])
    for c in range(ATTN_HEADS):
        sl = slice(c * LANES, (c + 1) * LANES)
        kr = rope(k[:, sl])
        k_ref[pl.ds(c, tm, stride=ATTN_HEADS), :] = kr
        kb_ref[:, sl] = kr.astype(BF16)
    v = _dot(h, w_ref[:, 2 * D_MODEL:3 * D_MODEL])
    for c in range(ATTN_HEADS):
        v_ref[pl.ds(c, tm, stride=ATTN_HEADS), :] = v[:, c * LANES:(c + 1) * LANES]
    vb_ref[...] = v.astype(BF16)


def _qkv_kernel_into(x_ref, g_ref, w_ref, cos_ref, slo_ref, shi_ref, k_all_ref, v_all_ref, *out_refs):
    del k_all_ref, v_all_ref
    _qkv_kernel(x_ref, g_ref, w_ref, cos_ref, slo_ref, shi_ref, *out_refs)


def _attn_qkv(x, g, wqkv, tables, seq_len, layer, kv_all):
    m = x.shape[0]
    tm = _row_tile(min(m, seq_len), 512)
    nt = seq_len // tm
    tab_spec = pl.BlockSpec((tm, LANES), lambda i: (i % nt, 0))
    row_spec = pl.BlockSpec((tm, D_MODEL), lambda i: (i, 0))
    slab_spec = pl.BlockSpec((None, tm * ATTN_HEADS, LANES), lambda i: (layer, i, 0))
    in_specs = [row_spec, _resident((1, D_MODEL)), _resident((D_MODEL, 3 * D_MODEL)), tab_spec, tab_spec, tab_spec]
    args = [x, g.reshape(1, D_MODEL), wqkv, *tables]
    if isinstance(kv_all, int):
        n_layers, body, aliases = kv_all, _qkv_kernel, {}
    else:
        n_layers, body, aliases = kv_all[0].shape[0], _qkv_kernel_into, {len(args): 1, len(args) + 1: 2}
        in_specs += [pl.BlockSpec(memory_space=pl.ANY)] * 2
        args += list(kv_all)
    stacked = jax.ShapeDtypeStruct((n_layers, m * ATTN_HEADS, LANES), F32)
    q, k_all, v_all, kb, vb = pl.pallas_call(
        body,
        grid=(m // tm,),
        in_specs=in_specs,
        out_specs=[row_spec, slab_spec, slab_spec, row_spec, row_spec],
        out_shape=[jax.ShapeDtypeStruct((m, D_MODEL), BF16), stacked, stacked,
                   jax.ShapeDtypeStruct((m, D_MODEL), BF16), jax.ShapeDtypeStruct((m, D_MODEL), BF16)],
        input_output_aliases=aliases,
        compiler_params=_cparams(1), name="attn_qkv",
    )(*args)
    return q, kb, vb, (k_all, v_all)


def _lambda_value(lam_ref, lam_init):
    lv = lam_ref[...]
    a = jnp.sum(lv[0:1] * lv[1:2], axis=1, keepdims=True)
    b = jnp.sum(lv[2:3] * lv[3:4], axis=1, keepdims=True)
    return jnp.exp(a) - jnp.exp(b) + lam_init


def _head_out(o1, o2, lam, subln, lam_init):
    o = o1 - lam * o2
    o = o * lax.rsqrt(jnp.mean(o * o, axis=-1, keepdims=True) + LN_EPS)
    return o * subln * (1.0 - lam_init)


def _attn_prompt_kernel(q_ref, kb_ref, vb_ref, lam_ref, sub_ref, o_ref, qq_ref, m_ref, l_ref, acc_ref,
                        *, tq, lam_init):
    i = pl.program_id(1)
    first = lax.broadcasted_iota(jnp.int32, (tq, LANES), 1) < ATTN_HEAD_DIM
    for h in range(ATTN_HEADS):
        q = q_ref[:, h * LANES:(h + 1) * LANES]
        zero = jnp.zeros_like(q)
        qq_ref[h] = jnp.concatenate([jnp.where(first, q, zero), jnp.where(first, zero, q)], axis=0)

    def block(j, diagonal):
        rows = pl.ds(pl.multiple_of(j * tq, tq), tq)
        if diagonal:
            r = lax.broadcasted_iota(jnp.int32, (2 * tq, tq), 0)
            c = lax.broadcasted_iota(jnp.int32, (2 * tq, tq), 1)
            keep = c <= jnp.where(r >= tq, r - tq, r)
        ones = jnp.ones((tq, LANES), BF16)
        for h in range(ATTN_HEADS):
            sl = slice(h * LANES, (h + 1) * LANES)
            s = _dot_nt(qq_ref[h], kb_ref[rows, sl])
            if diagonal:
                s = jnp.where(keep, s, NEG_INF)
                m_new = jnp.broadcast_to(jnp.max(s, axis=1, keepdims=True), (2 * tq, LANES))
            else:
                m_old = m_ref[h]
                m_new = jnp.maximum(m_old, jnp.max(s, axis=1, keepdims=True))
            p = jnp.exp2(s - jnp.concatenate([m_new] * (tq // LANES), axis=1))
            pv = _dot(p.astype(BF16), jnp.concatenate([vb_ref[rows, sl], ones], axis=1))
            if diagonal:
                l_ref[h] = pv[:, LANES:]
                acc_ref[h] = pv[:, :LANES]
            else:
                alpha = jnp.exp2(m_old - m_new)
                l_ref[h] = alpha * l_ref[h] + pv[:, LANES:]
                acc_ref[h] = alpha * acc_ref[h] + pv[:, :LANES]
            m_ref[h] = m_new

    def body(j, carry):
        block(j, False)
        return carry

    block(i, True)
    lax.fori_loop(0, i, body, 0)

    lam = _lambda_value(lam_ref, lam_init)
    for h in range(ATTN_HEADS):
        o = acc_ref[h] / l_ref[h]
        o_ref[:, h * LANES:(h + 1) * LANES] = _head_out(o[:tq], o[tq:], lam, sub_ref[...], lam_init).astype(BF16)


def _attn_prompt(q, kb, vb, lam_vec, subln, lam_init, batch, seq_len):
    tq = 256 if seq_len % 256 == 0 else seq_len
    nq = seq_len // tq
    q_spec = pl.BlockSpec((tq, D_MODEL), lambda b, i: (b * nq + i, 0))
    kv_spec = pl.BlockSpec((seq_len, D_MODEL), lambda b, i: (b, 0))
    return pl.pallas_call(
        functools.partial(_attn_prompt_kernel, tq=tq, lam_init=lam_init),
        grid=(batch, nq),
        in_specs=[q_spec, kv_spec, kv_spec, _resident((4, ATTN_HEAD_DIM)), _resident((1, ATTN_VDIM))],
        out_specs=q_spec,
        out_shape=jax.ShapeDtypeStruct((batch * seq_len, D_MODEL), BF16),
        scratch_shapes=[pltpu.VMEM((ATTN_HEADS, 2 * tq, LANES), BF16), pltpu.VMEM((ATTN_HEADS, 2 * tq, LANES), F32),
                        pltpu.VMEM((ATTN_HEADS, 2 * tq, LANES), F32), pltpu.VMEM((ATTN_HEADS, 2 * tq, LANES), F32)],
        compiler_params=_cparams(2), name="attn_prompt",
    )(q, kb, vb, lam_vec, subln.reshape(1, ATTN_VDIM))


DECODE_PAGES_PER_STEP = 8


def _attn_decode_kernel(pt_ref, q_ref, kn_ref, vn_ref, *refs, pages_per_step, lam_init):
    del pt_ref
    kc_refs, vc_refs = refs[:pages_per_step], refs[pages_per_step:2 * pages_per_step]
    lam_ref, sub_ref, o_ref, m_ref, l_ref, acc_ref = refs[2 * pages_per_step:]
    step = pl.program_id(1)

    @pl.when(step == 0)
    def _():
        m_ref[...] = jnp.full(m_ref.shape, NEG_INF, F32)
        l_ref[...] = jnp.zeros(l_ref.shape, F32)
        acc_ref[...] = jnp.zeros(acc_ref.shape, F32)

    q = q_ref[...]
    first = lax.broadcasted_iota(jnp.int32, (ATTN_HEADS, LANES), 1) < ATTN_HEAD_DIM
    qsel = jnp.concatenate([jnp.where(first, q, 0.0), jnp.where(first, 0.0, q)], axis=0)
    qb = qsel.astype(BF16)
    rows = kc_refs[0].shape[0] * ATTN_HEADS
    same_head = ((lax.broadcasted_iota(jnp.int32, (2 * ATTN_HEADS, rows), 1) & (ATTN_HEADS - 1))
                 == (lax.broadcasted_iota(jnp.int32, (2 * ATTN_HEADS, rows), 0) & (ATTN_HEADS - 1)))
    s = [jnp.where(same_head, _dot_nt(qb, kc_ref[...].reshape(rows, LANES).astype(BF16)), NEG_INF)
         for kc_ref in kc_refs]
    m_old = m_ref[...]
    m_new = m_old
    for si in s:
        m_new = jnp.maximum(m_new, jnp.max(si, axis=1, keepdims=True))
    alpha = jnp.exp2(m_old - m_new)
    p = [jnp.exp2(si - m_new) for si in s]
    l_new = alpha * l_ref[...]
    acc = alpha * acc_ref[...]
    for pi, vc_ref in zip(p, vc_refs):
        l_new = l_new + jnp.sum(pi, axis=1, keepdims=True)
        acc = acc + _dot(pi.astype(BF16), vc_ref[...].reshape(rows, LANES).astype(BF16))
    m_ref[...] = m_new
    l_ref[...] = l_new
    acc_ref[...] = acc

    @pl.when(step == pl.num_programs(1) - 1)
    def _():
        kn = jnp.concatenate([kn_ref[...]] * 2, axis=0)
        vn = jnp.concatenate([vn_ref[...]] * 2, axis=0)
        sn = jnp.sum(qsel * kn, axis=1, keepdims=True)
        m_fin = jnp.maximum(m_new, sn)
        a_fin = jnp.exp2(m_new - m_fin)
        pn = jnp.exp2(sn - m_fin)
        o = (a_fin * acc + pn * vn) / (a_fin * l_new + pn)
        lam = _lambda_value(lam_ref, lam_init)
        o_ref[...] = _head_out(o[:ATTN_HEADS], o[ATTN_HEADS:], lam, sub_ref[...], lam_init)


def _attn_decode(q, k_new, v_new, cache_k, cache_v, page_table, layer, lam_vec, subln, lam_init):
    batch, n_pages = page_table.shape
    page = cache_k.shape[2]
    pps = DECODE_PAGES_PER_STEP if n_pages % DECODE_PAGES_PER_STEP == 0 else 1
    tok_spec = pl.BlockSpec((None, ATTN_HEADS, LANES), lambda b, s, pt: (b, 0, 0))
    page_specs = [pl.BlockSpec((None, None, page, ATTN_HEADS, LANES),
                               lambda b, s, pt, i=i: (layer, pt[b, s * pps + i], 0, 0, 0)) for i in range(pps)]
    const2 = lambda b, s, pt: (0, 0)
    grid_spec = pltpu.PrefetchScalarGridSpec(
        num_scalar_prefetch=1,
        grid=(batch, n_pages // pps),
        in_specs=[tok_spec, tok_spec, tok_spec] + page_specs + page_specs
                 + [pl.BlockSpec((4, ATTN_HEAD_DIM), const2), pl.BlockSpec((1, ATTN_VDIM), const2)],
        out_specs=tok_spec,
        scratch_shapes=[pltpu.VMEM((2 * ATTN_HEADS, 1), F32), pltpu.VMEM((2 * ATTN_HEADS, 1), F32),
                        pltpu.VMEM((2 * ATTN_HEADS, LANES), F32)],
    )
    return pl.pallas_call(
        functools.partial(_attn_decode_kernel, pages_per_step=pps, lam_init=lam_init),
        grid_spec=grid_spec,
        out_shape=jax.ShapeDtypeStruct((batch, ATTN_HEADS, LANES), F32),
        compiler_params=_cparams(2), name="attn_decode",
    )(page_table, q, k_new, v_new, *([cache_k] * pps), *([cache_v] * pps), lam_vec, subln.reshape(1, ATTN_VDIM))


def _pair_ones():
    r = lax.broadcasted_iota(jnp.int32, (LANES, LANES), 0)
    c = lax.broadcasted_iota(jnp.int32, (LANES, LANES), 1)
    return ((r < RWKV_HEAD) == (c < RWKV_HEAD)).astype(BF16)


def _head_sums(z, ones2):
    return jnp.concatenate(
        [_dot(z[:, c * LANES:(c + 1) * LANES].astype(BF16), ones2) for c in range(D_MODEL // LANES)], axis=1)


def _rwkv_pre_kernel(x_ref, p_ref, g_ref, mix_ref, vec_ref, wr_ref, wk_ref, wv_ref, w1_ref, w2_ref, a1_ref, a2_ref,
                     g1_ref, g2_ref, r_o, lw_o, k_o, v_o, kk_o, b_o, gate_o, bonus_o, *, seq_tiles):
    tm = x_ref.shape[0]
    g = g_ref[...]
    hn = _rms(x_ref[...], g, NORM_EPS)
    if seq_tiles:
        hp = _rms(p_ref[7:8, :], g, NORM_EPS)
        hp = jnp.where(pl.program_id(0) % seq_tiles == 0, 0.0, hp)
        row = lax.broadcasted_iota(jnp.int32, (tm, D_MODEL), 0)
        prev = jnp.where(row == 0, hp, pltpu.roll(hn, 1, 0))
    else:
        prev = p_ref[...]
    xx = prev - hn
    mix = mix_ref[...]
    vec = vec_ref[...]
    w0, a0, k_k, k_a, r_k = (vec[c:c + 1] for c in range(5))

    def mixed(c):
        return (hn + xx * mix[c:c + 1]).astype(BF16)

    r = _dot(mixed(0), wr_ref[...])
    z = w0 + _dot(jnp.tanh(_dot(mixed(1), w1_ref[...])).astype(BF16), w2_ref[...])
    k = _dot(mixed(2), wk_ref[...])
    v = _dot(mixed(3), wv_ref[...])
    a = jax.nn.sigmoid(a0 + _dot(_dot(mixed(4), a1_ref[...]).astype(BF16), a2_ref[...]))
    gate = _dot(jax.nn.sigmoid(_dot(mixed(5), g1_ref[...])).astype(BF16), g2_ref[...])
    softplus = jnp.maximum(-z, 0.0) + jnp.log(1.0 + jnp.exp(-jnp.abs(z)))
    lw = -jnp.exp(-softplus - 0.5)
    ones2 = _pair_ones()
    kk = k * k_k
    kk = kk * lax.rsqrt(jnp.maximum(_head_sums(kk * kk, ones2), 1e-24))
    k = k * (1.0 + (a - 1.0) * k_a)
    r_o[...] = r
    lw_o[...] = lw
    k_o[...] = k
    v_o[...] = v
    kk_o[...] = kk
    b_o[...] = kk * a
    gate_o[...] = gate
    bonus_o[...] = _head_sums(r * k * r_k, ones2) * v


def _rwkv_pre(x, prev, g, mix, vec, mats, seq_len):
    m = x.shape[0]
    if prev is None:
        tm = _row_tile(seq_len, 256)
        seq_tiles = seq_len // tm
        p_arg = x
        p_spec = pl.BlockSpec((8, D_MODEL), lambda i: (jnp.maximum(i * (tm // 8) - 1, 0), 0))
    else:
        tm = _row_tile(m, 256)
        seq_tiles = 0
        p_arg = prev
        p_spec = pl.BlockSpec((tm, D_MODEL), lambda i: (i, 0))
    row_spec = pl.BlockSpec((tm, D_MODEL), lambda i: (i, 0))
    return pl.pallas_call(
        functools.partial(_rwkv_pre_kernel, seq_tiles=seq_tiles),
        grid=(m // tm,),
        in_specs=[row_spec, p_spec, _resident((1, D_MODEL)), _resident(mix.shape), _resident(vec.shape)]
                 + [_resident(w.shape) for w in mats],
        out_specs=[row_spec] * 8,
        out_shape=[jax.ShapeDtypeStruct((m, D_MODEL), F32)] * 8,
        compiler_params=_cparams(1), name="rwkv_pre",
    )(x, p_arg, g.reshape(1, D_MODEL), mix, vec, *mats)


def _wkv_kernel(r_ref, lw_ref, k_ref, v_ref, kk_ref, b_ref, gate_ref, bonus_ref, lnw_ref, lnb_ref,
                y_ref, sout_ref, s_ref, ar_ref, bk_ref, tinv_ref, aab_ref, rb_ref, x0_ref, y0_ref, pc_ref,
                *, n_chunks):
    c_len = WKV_CHUNK
    n_pairs = RWKV_HEADS // 2
    pairs = range(n_pairs)
    t = pl.program_id(1)

    @pl.when(t == 0)
    def _():
        s_ref[...] = jnp.zeros(s_ref.shape, F32)

    rr = lax.broadcasted_iota(jnp.int32, (c_len, c_len), 0)
    cc = lax.broadcasted_iota(jnp.int32, (c_len, c_len), 1)
    ltri = (rr >= cc).astype(F32)
    r2 = lax.broadcasted_iota(jnp.int32, (LANES, LANES), 0)
    c2 = lax.broadcasted_iota(jnp.int32, (LANES, LANES), 1)
    strict = r2 > c2
    incl = r2 >= c2
    eye = (r2 == c2).astype(F32)
    ones2 = _pair_ones()
    first = lax.broadcasted_iota(jnp.int32, (c_len, LANES), 1) < RWKV_HEAD
    lanes = [slice(p * LANES, (p + 1) * LANES) for p in pairs]

    def stack(x):
        zero = jnp.zeros_like(x)
        return jnp.concatenate([jnp.where(first, x, zero), jnp.where(first, zero, x)], axis=0)

    def prepare(c, carry):
        rows = pl.ds(pl.multiple_of(c * c_len, c_len), c_len)
        lw = lw_ref[rows, :]
        cum = jnp.dot(ltri, lw, preferred_element_type=F32, precision=HIGHEST)
        e_in = jnp.exp(cum)
        e_neg = jnp.exp(-cum)
        at_all = (-(kk_ref[rows, :] * jnp.exp(cum - lw))).astype(BF16)
        rt_all = (r_ref[rows, :] * e_in).astype(BF16)
        bt_all = (b_ref[rows, :] * e_neg).astype(BF16)
        kt_all = (k_ref[rows, :] * e_neg).astype(BF16)
        v_all = v_ref[rows, :].astype(BF16)
        ar_ref[c] = jnp.concatenate([at_all, rt_all], axis=0)
        pc_ref[c] = e_in[c_len - 8:c_len, :]
        bk = [jnp.concatenate([stack(bt_all[:, sl]), stack(kt_all[:, sl])], axis=0) for sl in lanes]
        ar = [jnp.concatenate([stack(at_all[:, sl]), stack(rt_all[:, sl])], axis=0) for sl in lanes]
        gram = [_dot_nt(ar[p], bk[p]) for p in pairs]
        vs = [stack(v_all[:, sl]) for sl in lanes]
        a_ab = [jnp.where(strict, gram[p][:LANES, :LANES], 0.0) for p in pairs]
        a_ak = [jnp.where(strict, gram[p][:LANES, LANES:], 0.0).astype(BF16) for p in pairs]
        r_k = [jnp.where(incl, gram[p][LANES:, LANES:], 0.0).astype(BF16) for p in pairs]
        for p in pairs:
            bk_ref[c, p] = bk[p]
            rb_ref[c, p] = jnp.where(incl, gram[p][LANES:, :LANES], 0.0).astype(BF16)
        x0 = [_dot(a_ak[p], vs[p]) for p in pairs]
        y0 = [_dot(r_k[p], vs[p]) for p in pairs]
        for p in pairs:
            x0_ref[c, p] = x0[p]
            y0_ref[c, p] = y0[p]
        tinv = [eye + a_ab[p] for p in pairs]
        pw = [a_ab[p].astype(BF16) for p in pairs]
        for p in pairs:
            aab_ref[c, p] = pw[p]
        pw = [_dot(pw[p], pw[p]).astype(BF16) for p in pairs]
        for _ in range(int(math.log2(c_len)) - 2):
            both = [_dot(pw[p], jnp.concatenate([tinv[p].astype(BF16), pw[p]], axis=1)) for p in pairs]
            tinv = [tinv[p] + both[p][:, :LANES] for p in pairs]
            pw = [both[p][:, LANES:].astype(BF16) for p in pairs]
        for p in pairs:
            tinv_ref[c, p] = (tinv[p] + _dot(pw[p], tinv[p].astype(BF16))).astype(BF16)
        return carry

    lax.fori_loop(0, n_chunks, prepare, 0)

    def advance(c, carry):
        rows = pl.ds(pl.multiple_of(c * c_len, c_len), c_len)
        ar = ar_ref[c]
        s = [s_ref[p] for p in pairs]
        m1 = [_dot_nt(ar[:, lanes[p]], s[p].astype(BF16)) for p in pairs]
        x = [stack(m1[p][:c_len]) + x0_ref[c, p] for p in pairs]
        u = [_dot(tinv_ref[c, p], x[p].astype(BF16)) for p in pairs]
        for _ in range(WKV_REFINE_STEPS):
            res = [x[p] - u[p] + _dot(aab_ref[c, p], u[p].astype(BF16)) for p in pairs]
            u = [u[p] + _dot(tinv_ref[c, p], res[p].astype(BF16)) for p in pairs]
        ub = [u[p].astype(BF16) for p in pairs]
        vs = [stack(v_ref[rows, lanes[p]].astype(BF16)) for p in pairs]
        y2 = [stack(m1[p][c_len:]) + y0_ref[c, p] + _dot(rb_ref[c, p], ub[p]) for p in pairs]
        upd = [_dot_tn(jnp.concatenate([ub[p], vs[p]], axis=0), bk_ref[c, p]) for p in pairs]
        decay = pc_ref[c]
        for p in pairs:
            s_ref[p] = (s[p] + upd[p]) * decay[7:8, lanes[p]]
        y = [y2[p][:c_len] + y2[p][c_len:] for p in pairs]
        mu = [_dot(y[p].astype(BF16), ones2) * (1.0 / RWKV_HEAD) for p in pairs]
        d = [y[p] - mu[p] for p in pairs]
        var = [_dot((d[p] * d[p]).astype(BF16), ones2) * (1.0 / RWKV_HEAD) for p in pairs]
        for p in pairs:
            sl = lanes[p]
            yn = d[p] * lax.rsqrt(var[p] + RWKV_GN_EPS) * lnw_ref[:, sl] + lnb_ref[:, sl]
            y_ref[rows, sl] = ((yn + bonus_ref[rows, sl]) * gate_ref[rows, sl]).astype(y_ref.dtype)
        return carry

    lax.fori_loop(0, n_chunks, advance, 0)

    @pl.when(t == pl.num_programs(1) - 1)
    def _():
        sout_ref[...] = s_ref[...]


def _wkv_prompt(streams, ln_w, ln_b, batch, seq_len):
    tw = _row_tile(seq_len, 256)
    nt = seq_len // tw
    nc = tw // WKV_CHUNK
    row_spec = pl.BlockSpec((tw, D_MODEL), lambda b, t: (b * nt + t, 0))
    n_pairs = RWKV_HEADS // 2
    pair_mat = lambda rows, cols, dt: pltpu.VMEM((nc, n_pairs, rows, cols), dt)
    return pl.pallas_call(
        functools.partial(_wkv_kernel, n_chunks=nc),
        grid=(batch, nt),
        in_specs=[row_spec] * 8 + [_resident((1, D_MODEL)), _resident((1, D_MODEL))],
        out_specs=[row_spec, pl.BlockSpec((None, n_pairs, LANES, LANES), lambda b, t: (b, 0, 0, 0))],
        out_shape=[jax.ShapeDtypeStruct((batch * seq_len, D_MODEL), BF16),
                   jax.ShapeDtypeStruct((batch, n_pairs, LANES, LANES), F32)],
        scratch_shapes=[pltpu.VMEM((n_pairs, LANES, LANES), F32),
                        pltpu.VMEM((nc, 2 * WKV_CHUNK, D_MODEL), BF16),
                        pair_mat(4 * WKV_CHUNK, LANES, BF16),
                        pair_mat(LANES, LANES, BF16), pair_mat(LANES, LANES, BF16), pair_mat(LANES, LANES, BF16),
                        pair_mat(LANES, LANES, F32), pair_mat(LANES, LANES, F32),
                        pltpu.VMEM((nc, 8, D_MODEL), F32)],
        compiler_params=_cparams(2), name="wkv_prompt",
    )(*streams, ln_w.reshape(1, D_MODEL), ln_b.reshape(1, D_MODEL))


def _wkv_step_kernel(s_ref, r_ref, lw_ref, k_ref, v_ref, kk_ref, b_ref, gate_ref, bonus_ref, lnw_ref, lnb_ref,
                     y_ref, sout_ref):
    n = RWKV_HEAD
    s = s_ref[...]
    eye = (lax.broadcasted_iota(jnp.int32, (n, n), 0) == lax.broadcasted_iota(jnp.int32, (n, n), 1)).astype(F32)
    sa = -jnp.sum(s * kk_ref[...], axis=-1, keepdims=True)
    vcol = jnp.sum(eye * v_ref[...], axis=-1, keepdims=True)
    s = s * jnp.exp(lw_ref[...]) + sa * b_ref[...] + vcol * k_ref[...]
    sout_ref[...] = s
    ycol = jnp.sum(s * r_ref[...], axis=-1, keepdims=True)
    y = jnp.sum(eye * ycol, axis=-2, keepdims=True)
    mu = jnp.mean(y, axis=-1, keepdims=True)
    d = y - mu
    var = jnp.mean(d * d, axis=-1, keepdims=True)
    yn = d * lax.rsqrt(var + RWKV_GN_EPS) * lnw_ref[...] + lnb_ref[...]
    y_ref[...] = (yn + bonus_ref[...]) * gate_ref[...]


def _wkv_step(state, streams, ln_w, ln_b):
    batch = state.shape[0]
    nb = 8 if batch % 8 == 0 else batch
    vec = lambda z: z.reshape(batch, RWKV_HEADS, 1, RWKV_HEAD)
    st_spec = pl.BlockSpec((nb, RWKV_HEADS, RWKV_HEAD, RWKV_HEAD), lambda i: (i, 0, 0, 0))
    vec_spec = pl.BlockSpec((nb, RWKV_HEADS, 1, RWKV_HEAD), lambda i: (i, 0, 0, 0))
    ln_spec = pl.BlockSpec((1, RWKV_HEADS, 1, RWKV_HEAD), lambda i: (0, 0, 0, 0))
    y, s_new = pl.pallas_call(
        _wkv_step_kernel,
        grid=(batch // nb,),
        in_specs=[st_spec] + [vec_spec] * 8 + [ln_spec, ln_spec],
        out_specs=[vec_spec, st_spec],
        out_shape=[jax.ShapeDtypeStruct((batch, RWKV_HEADS, 1, RWKV_HEAD), F32),
                   jax.ShapeDtypeStruct(state.shape, F32)],
        compiler_params=_cparams(1), name="wkv_step",
    )(state, *[vec(z) for z in streams], ln_w.reshape(1, RWKV_HEADS, 1, RWKV_HEAD),
      ln_b.reshape(1, RWKV_HEADS, 1, RWKV_HEAD))
    return y.reshape(batch, D_MODEL), s_new


CONV_HALO = 32
CONV_TAP_ROWS = 64


def _conv_glu_kernel(x_ref, g_ref, w_ref, b_ref, u_ref):
    h = _rms(x_ref[...], g_ref[...], NORM_EPS).astype(BF16)
    b = b_ref[...]
    a = _dot(h, w_ref[:, :D_MODEL]) + b[:, :D_MODEL]
    gate = _dot(h, w_ref[:, D_MODEL:]) + b[:, D_MODEL:]
    u_ref[...] = a * jax.nn.sigmoid(gate)


def _conv_glu(x, g, w1, b1):
    m = x.shape[0]
    tm = _row_tile(m, 512)
    row_spec = pl.BlockSpec((tm, D_MODEL), lambda i: (i, 0))
    return pl.pallas_call(
        _conv_glu_kernel,
        grid=(m // tm,),
        in_specs=[row_spec, _resident((1, D_MODEL)), _resident((D_MODEL, 2 * D_MODEL)), _resident((1, 2 * D_MODEL))],
        out_specs=row_spec,
        out_shape=jax.ShapeDtypeStruct((m, D_MODEL), F32),
        compiler_params=_cparams(1), name="conv_glu",
    )(x, g.reshape(1, D_MODEL), w1, b1.reshape(1, 2 * D_MODEL))


def _ln_silu(y, ln_w, ln_b):
    mu = jnp.mean(y, axis=-1, keepdims=True)
    d = y - mu
    var = jnp.mean(d * d, axis=-1, keepdims=True)
    yn = d * lax.rsqrt(var + LN_EPS) * ln_w + ln_b
    return yn * jax.nn.sigmoid(yn)


def _conv_dw_kernel(u_ref, halo_ref, dw_ref, vec_ref, w2_ref, res_ref, o_ref, ext_ref, s_ref, *, seq_tiles):
    tm = u_ref.shape[0]
    ext_ref[0:CONV_HALO, :] = jnp.where(pl.program_id(0) % seq_tiles == 0, 0.0, halo_ref[...])
    ext_ref[CONV_HALO:CONV_HALO + tm, :] = u_ref[...]
    vec = vec_ref[...]
    dw_b, ln_w, ln_b, b2 = (vec[c:c + 1] for c in range(4))
    dw = dw_ref[...]
    phases = [[] for _ in range(8)]
    for k in range(CONV_WIDTH):
        off = CONV_HALO - CONV_BUF + k
        phases[off % 8].append((k, off - off % 8))
    for c in range(tm // CONV_TAP_ROWS):
        base = c * CONV_TAP_ROWS
        acc = dw_b
        for b, taps in enumerate(phases):
            part = None
            n_rows = CONV_TAP_ROWS + (8 if b else 0)
            for k, off in taps:
                term = dw[k:k + 1] * ext_ref[base + off:base + off + n_rows, :]
                part = term if part is None else part + term
            acc = acc + part[b:b + CONV_TAP_ROWS]
        s_ref[base:base + CONV_TAP_ROWS, :] = _ln_silu(acc, ln_w, ln_b).astype(BF16)
    o_ref[...] = res_ref[...] + _dot(s_ref[...], w2_ref[...]) + b2


def _conv_dw(u, dw, vec, w2, res, seq_len):
    m = u.shape[0]
    tm = _row_tile(seq_len, 256)
    per = tm // CONV_HALO
    row_spec = pl.BlockSpec((tm, D_MODEL), lambda i: (i, 0))
    halo_spec = pl.BlockSpec((CONV_HALO, D_MODEL), lambda i: (jnp.maximum(i * per - 1, 0), 0))
    return pl.pallas_call(
        functools.partial(_conv_dw_kernel, seq_tiles=seq_len // tm),
        grid=(m // tm,),
        in_specs=[row_spec, halo_spec, _resident(dw.shape), _resident(vec.shape), _resident((D_MODEL, D_MODEL)), row_spec],
        out_specs=row_spec,
        out_shape=jax.ShapeDtypeStruct((m, D_MODEL), F32),
        scratch_shapes=[pltpu.VMEM((CONV_HALO + tm, D_MODEL), F32), pltpu.VMEM((tm, D_MODEL), BF16)],
        compiler_params=_cparams(1), name="conv_dw",
    )(u, u, dw, vec, w2, res)


def _conv_step_kernel(buf_ref, u_ref, dw_ref, vec_ref, s_ref):
    vec = vec_ref[...]
    dw = dw_ref[...]
    y = (jnp.sum(buf_ref[...] * dw[0:CONV_BUF], axis=1, keepdims=True)
         + u_ref[...] * dw[CONV_BUF:CONV_WIDTH] + vec[0:1])
    s_ref[...] = _ln_silu(y, vec[1:2], vec[2:3])


def _conv_step(buf, u, dw, vec):
    batch = buf.shape[0]
    nb = 8 if batch % 8 == 0 else batch
    one_spec = pl.BlockSpec((nb, 1, D_MODEL), lambda i: (i, 0, 0))
    return pl.pallas_call(
        _conv_step_kernel,
        grid=(batch // nb,),
        in_specs=[pl.BlockSpec((nb, CONV_BUF, D_MODEL), lambda i: (i, 0, 0)), one_spec,
                  _resident(dw.shape), _resident(vec.shape)],
        out_specs=one_spec,
        out_shape=jax.ShapeDtypeStruct((batch, 1, D_MODEL), F32),
        compiler_params=_cparams(1), name="conv_step",
    )(buf, u.reshape(batch, 1, D_MODEL), dw, vec).reshape(batch, D_MODEL)


def _rows8(*rows):
    pad = [jnp.zeros((D_MODEL,), F32)] * (8 - len(rows))
    return jnp.stack([r.reshape(D_MODEL).astype(F32) for r in rows] + pad)


def kernel(x_prompt, x_sample, cache_k, cache_v, page_table, state_wkv, state_shift, state_conv, norm_mix, norm_ffn, norm_final, attn_w_qkv, attn_w_o, attn_lambda, attn_subln, rwkv_mix, rwkv_w0, rwkv_w1, rwkv_w2, rwkv_a0, rwkv_a1, rwkv_a2, rwkv_g1, rwkv_g2, rwkv_k_k, rwkv_k_a, rwkv_r_k, rwkv_wr, rwkv_wk, rwkv_wv, rwkv_wo, rwkv_ln_w, rwkv_ln_b, conv_w1, conv_b1, conv_dw, conv_dw_b, conv_ln_w, conv_ln_b, conv_w2, conv_b2, ffn_w_up, ffn_w_down):
    bp, seq, d = x_prompt.shape
    bs = x_sample.shape[0]
    n_layers_attn, n_pool, page, heads, vdim = cache_k.shape
    past_len = page_table.shape[1] * page
    xp = x_prompt.reshape(bp * seq, d)
    xs = x_sample.reshape(bs, d)
    tabs_p = _rope_tables(jnp.arange(seq, dtype=jnp.int32))
    tabs_s = _rope_tables(jnp.full((bs,), past_len, jnp.int32))
    no_bias = jnp.zeros((d,), F32)
    kv_p = kv_s = n_layers_attn
    wkv_p, wkv_s, sh_p, sh_s, cv_p, cv_s = ([] for _ in range(6))
    for i in range(DEPTH):
        j = i // N_MIXERS
        kind = i % N_MIXERS
        if kind == 0:
            lam_init = 0.8 - 0.6 * math.exp(-0.3 * i)
            wqkv = attn_w_qkv[j].astype(BF16)
            wo = attn_w_o[j].astype(BF16)
            q, kb, vb, kv_p = _attn_qkv(xp, norm_mix[i], wqkv, tabs_p, seq, j, kv_p)
            o = _attn_prompt(q, kb, vb, attn_lambda[j], attn_subln[j], lam_init, bp, seq)
            xp = _matmul_res(o, wo, no_bias, xp)
            q_s, _, _, kv_s = _attn_qkv(xs, norm_mix[i], wqkv, tabs_s, bs, j, kv_s)
            tok = lambda z: z.reshape(bs, heads, vdim)
            o_s = _attn_decode(tok(q_s.astype(F32)), tok(kv_s[0][j]), tok(kv_s[1][j]), cache_k, cache_v, page_table,
                               j, attn_lambda[j], attn_subln[j], lam_init)
            xs = _matmul_res(o_s.reshape(bs, d), wo, no_bias, xs)
        elif kind == 1:
            mats = [w[j].astype(BF16) for w in (rwkv_wr, rwkv_wk, rwkv_wv, rwkv_w1, rwkv_w2, rwkv_a1, rwkv_a2,
                                                rwkv_g1, rwkv_g2)]
            vec = _rows8(rwkv_w0[j], rwkv_a0[j], rwkv_k_k[j], rwkv_k_a[j], rwkv_r_k[j])
            wo = rwkv_wo[j].astype(BF16)
            sh_p.append(_final_norm(xp.reshape(bp, seq, d)[:, -1], norm_mix[i]))
            sh_s.append(_final_norm(xs, norm_mix[i]))
            streams = _rwkv_pre(xp, None, norm_mix[i], rwkv_mix[j], vec, mats, seq)
            y, s_pairs = _wkv_prompt(streams, rwkv_ln_w[j], rwkv_ln_b[j], bp, seq)
            xp = _matmul_res(y, wo, no_bias, xp)
            s6 = s_pairs.reshape(bp, RWKV_HEADS // 2, 2, RWKV_HEAD, 2, RWKV_HEAD)
            wkv_p.append(jnp.stack([s6[:, :, 0, :, 0, :], s6[:, :, 1, :, 1, :]], axis=2)
                         .reshape(bp, RWKV_HEADS, RWKV_HEAD, RWKV_HEAD))
            streams = _rwkv_pre(xs, state_shift[j], norm_mix[i], rwkv_mix[j], vec, mats, 1)
            y_s, s_new = _wkv_step(state_wkv[j], streams, rwkv_ln_w[j], rwkv_ln_b[j])
            xs = _matmul_res(y_s, wo, no_bias, xs)
            wkv_s.append(s_new)
        else:
            w1 = conv_w1[j].astype(BF16)
            w2 = conv_w2[j].astype(BF16)
            dw = jnp.concatenate([conv_dw[j], jnp.zeros((1, d), F32)], axis=0)
            vec = _rows8(conv_dw_b[j], conv_ln_w[j], conv_ln_b[j], conv_b2[j])
            u = _conv_glu(xp, norm_mix[i], w1, conv_b1[j])
            xp = _conv_dw(u, dw, vec, w2, xp, seq)
            cv_p.append(u.reshape(bp, seq, d)[:, seq - CONV_BUF:])
            u_s = _conv_glu(xs, norm_mix[i], w1, conv_b1[j])
            xs = _matmul_res(_conv_step(state_conv[j], u_s, dw, vec), w2, conv_b2[j], xs)
            cv_s.append(jnp.concatenate([state_conv[j][:, 1:], u_s[:, None, :]], axis=1))
        wup = ffn_w_up[i].astype(BF16)
        wdn = ffn_w_down[i].astype(BF16)
        xp = _ffn(xp, norm_ffn[i], wup, wdn)
        xs = _ffn(xs, norm_ffn[i], wup, wdn)
    return (_final_norm(xp, norm_final).reshape(bp, seq, d), _final_norm(xs, norm_final).reshape(bs, 1, d),
            kv_p[0].reshape(n_layers_attn, bp, seq, heads, vdim), kv_p[1].reshape(n_layers_attn, bp, seq, heads, vdim),
            kv_s[0].reshape(n_layers_attn, bs, 1, heads, vdim), kv_s[1].reshape(n_layers_attn, bs, 1, heads, vdim),
            jnp.stack(wkv_p), jnp.stack(wkv_s),
            jnp.stack(sh_p), jnp.stack(sh_s), jnp.stack(cv_p), jnp.stack(cv_s))
```

```python
import functools
import math

import jax
import jax.numpy as jnp
from jax import lax
from jax.experimental import pallas as pl
from jax.experimental.pallas import tpu as pltpu

D_MODEL = 1024
DEPTH = 4
N_MIXERS = 3
ATTN_HEAD_DIM = 64
ATTN_HEADS = 8
ATTN_VDIM = 128
ROPE_THETA = 10000.0
NEG_INF = -1e30
RWKV_HEAD = 64
RWKV_HEADS = 16
RWKV_GN_EPS = 64e-5
CONV_WIDTH = 31
CONV_BUF = CONV_WIDTH - 1
LN_EPS = 1e-5
FFN_HIDDEN = 4 * D_MODEL
NORM_EPS = 1e-6

LANES = 128
VMEM_LIMIT = 56 * 1024 * 1024
WKV_CHUNK = 64
WKV_REFINE_STEPS = 1
Q_SCALE = ATTN_HEAD_DIM ** -0.5 * math.log2(math.e)

F32 = jnp.float32
BF16 = jnp.bfloat16
HIGHEST = lax.Precision.HIGHEST


def _cparams(n_axes):
    return pltpu.CompilerParams(dimension_semantics=("arbitrary",) * n_axes, vmem_limit_bytes=VMEM_LIMIT)


def _resident(shape):
    nd = len(shape)
    return pl.BlockSpec(shape, lambda *_: (0,) * nd, pipeline_mode=pl.Buffered(1))


def _dot(a, b):
    return jnp.dot(a, b, preferred_element_type=F32)


def _dot_nt(a, b, precision=None):
    return lax.dot_general(a, b, (((1,), (1,)), ((), ())), preferred_element_type=F32, precision=precision)


def _dot_tn(a, b, precision=None):
    return lax.dot_general(a, b, (((0,), (0,)), ((), ())), preferred_element_type=F32, precision=precision)


def _rms(x, g, eps):
    return x * lax.rsqrt(jnp.mean(x * x, axis=-1, keepdims=True) + eps) * g


def _row_tile(m, want):
    return want if m % want == 0 else m


def _ffn_kernel(*refs, chunk, with_mixer, with_out_norm):
    if with_out_norm:
        *refs, gout_ref, o_ref = refs
        refs.append(o_ref)
    if with_mixer:
        a_ref, wo_ref, b_ref, x_ref, g_ref, wup_ref, wdn_ref, o_ref = refs
        x = x_ref[...] + _dot(a_ref[...].astype(BF16), wo_ref[...]) + b_ref[...]
    else:
        x_ref, g_ref, wup_ref, wdn_ref, o_ref = refs
        x = x_ref[...]
    h = _rms(x, g_ref[...], NORM_EPS).astype(BF16)
    acc = x
    for c in range(FFN_HIDDEN // chunk):
        u = _dot(h, wup_ref[:, c * chunk:(c + 1) * chunk])
        u = jnp.maximum(u, 0.0)
        acc = acc + _dot((u * u).astype(BF16), wdn_ref[c * chunk:(c + 1) * chunk, :])
    o_ref[...] = _rms(acc, gout_ref[...], NORM_EPS) if with_out_norm else acc


def _ffn(x, g, wup, wdn, mixer=None, out_norm=None):
    m = x.shape[0]
    tm = _row_tile(m, 512)
    row_spec = pl.BlockSpec((tm, D_MODEL), lambda i: (i, 0))
    in_specs = [row_spec, _resident((1, D_MODEL)), _resident((D_MODEL, FFN_HIDDEN)), _resident((FFN_HIDDEN, D_MODEL))]
    args = [x, g.reshape(1, D_MODEL), wup, wdn]
    if mixer is not None:
        a, wo, bias = mixer
        in_specs = [pl.BlockSpec((tm, a.shape[1]), lambda i: (i, 0)), _resident(wo.shape), _resident((1, D_MODEL))] + in_specs
        args = [a, wo, bias.reshape(1, D_MODEL)] + args
    if out_norm is not None:
        in_specs.append(_resident((1, D_MODEL)))
        args.append(out_norm.reshape(1, D_MODEL))
    return pl.pallas_call(
        functools.partial(_ffn_kernel, chunk=1024, with_mixer=mixer is not None, with_out_norm=out_norm is not None),
        grid=(m // tm,),
        in_specs=in_specs,
        out_specs=row_spec,
        out_shape=jax.ShapeDtypeStruct((m, D_MODEL), F32),
        compiler_params=_cparams(1), name="ffn",
    )(*args)


def _norm_kernel(x_ref, g_ref, o_ref):
    o_ref[...] = _rms(x_ref[...], g_ref[...], NORM_EPS)


def _norm_rows(x, g):
    m = x.shape[0]
    tm = _row_tile(m, 1024)
    return pl.pallas_call(
        _norm_kernel,
        grid=(m // tm,),
        in_specs=[pl.BlockSpec((tm, D_MODEL), lambda i: (i, 0)), _resident((1, D_MODEL))],
        out_specs=pl.BlockSpec((tm, D_MODEL), lambda i: (i, 0)),
        out_shape=jax.ShapeDtypeStruct((m, D_MODEL), F32),
        compiler_params=_cparams(1), name="norm_rows",
    )(x, g.reshape(1, D_MODEL))


def _rope_tables(pos):
    half = ATTN_HEAD_DIM // 2
    inv = ROPE_THETA ** (-jnp.arange(half, dtype=F32) / half)
    ang = pos.astype(F32)[:, None] * inv[None, :]
    cos, sin = jnp.cos(ang), jnp.sin(ang)
    zero = jnp.zeros_like(sin)
    cos_t = jnp.tile(cos, (1, 4))
    sin_lo = jnp.tile(jnp.concatenate([-sin, zero], axis=1), (1, 2))
    sin_hi = jnp.tile(jnp.concatenate([zero, sin], axis=1), (1, 2))
    return cos_t, sin_lo, sin_hi


def _qkv_kernel(x_ref, g_ref, w_ref, cos_ref, slo_ref, shi_ref, q_ref, k_ref, v_ref, kb_ref, vb_ref):
    h = _rms(x_ref[...], g_ref[...], NORM_EPS).astype(BF16)
    cos, slo, shi = cos_ref[...], slo_ref[...], shi_ref[...]

    def rope(z):
        return z * cos + pltpu.roll(z, LANES - 32, 1) * slo + pltpu.roll(z, 32, 1) * shi

    q = _dot(h, w_ref[:, 0:D_MODEL])
    for c in range(ATTN_HEADS):
        sl = slice(c * LANES, (c + 1) * LANES)
        q_ref[:, sl] = (rope(q[:, sl]) * Q_SCALE).astype(BF16)
    tm = x_ref.shape[0]
    k = _dot(h, w_ref[:, D_MODEL:2 * D_MODEL])
    for c in range(ATTN_HEADS):
        sl = slice(c * LANES, (c + 1) * LANES)
        kr = rope(k[:, sl])
        k_ref[pl.ds(c, tm, stride=ATTN_HEADS), :] = kr
        kb_ref[:, sl] = kr.astype(BF16)
    v = _dot(h, w_ref[:, 2 * D_MODEL:3 * D_MODEL])
    for c in range(ATTN_HEADS):
        v_ref[pl.ds(c, tm, stride=ATTN_HEADS), :] = v[:, c * LANES:(c + 1) * LANES]
    vb_ref[...] = v.astype(BF16)


def _qkv_kernel_into(x_ref, g_ref, w_ref, cos_ref, slo_ref, shi_ref, k_all_ref, v_all_ref, *out_refs):
    del k_all_ref, v_all_ref
    _qkv_kernel(x_ref, g_ref, w_ref, cos_ref, slo_ref, shi_ref, *out_refs)


def _attn_qkv(x, g, wqkv, tables, seq_len, layer, kv_all):
    m = x.shape[0]
    tm = _row_tile(min(m, seq_len), 512)
    nt = seq_len // tm
    tab_spec = pl.BlockSpec((tm, LANES), lambda i: (i % nt, 0))
    row_spec = pl.BlockSpec((tm, D_MODEL), lambda i: (i, 0))
    slab_spec = pl.BlockSpec((None, tm * ATTN_HEADS, LANES), lambda i: (layer, i, 0))
    in_specs = [row_spec, _resident((1, D_MODEL)), _resident((D_MODEL, 3 * D_MODEL)), tab_spec, tab_spec, tab_spec]
    args = [x, g.reshape(1, D_MODEL), wqkv, *tables]
    if isinstance(kv_all, int):
        n_layers, body, aliases = kv_all, _qkv_kernel, {}
    else:
        n_layers, body, aliases = kv_all[0].shape[0], _qkv_kernel_into, {len(args): 1, len(args) + 1: 2}
        in_specs += [pl.BlockSpec(memory_space=pl.ANY)] * 2
        args += list(kv_all)
    stacked = jax.ShapeDtypeStruct((n_layers, m * ATTN_HEADS, LANES), F32)
    q, k_all, v_all, kb, vb = pl.pallas_call(
        body,
        grid=(m // tm,),
        in_specs=in_specs,
        out_specs=[row_spec, slab_spec, slab_spec, row_spec, row_spec],
        out_shape=[jax.ShapeDtypeStruct((m, D_MODEL), BF16), stacked, stacked,
                   jax.ShapeDtypeStruct((m, D_MODEL), BF16), jax.ShapeDtypeStruct((m, D_MODEL), BF16)],
        input_output_aliases=aliases,
        compiler_params=_cparams(1), name="attn_qkv",
    )(*args)
    return q, kb, vb, (k_all, v_all)


def _lambda_value(lam_ref, lam_init):
    lv = lam_ref[...]
    a = jnp.sum(lv[0:1] * lv[1:2], axis=1, keepdims=True)
    b = jnp.sum(lv[2:3] * lv[3:4], axis=1, keepdims=True)
    return jnp.exp(a) - jnp.exp(b) + lam_init


def _head_out(o1, o2, lam, subln, lam_init):
    o = o1 - lam * o2
    o = o * lax.rsqrt(jnp.mean(o * o, axis=-1, keepdims=True) + LN_EPS)
    return o * subln * (1.0 - lam_init)


def _attn_prompt_kernel(q_ref, kb_ref, vb_ref, lam_ref, sub_ref, o_ref, qq_ref, m_ref, l_ref, acc_ref,
                        *, tq, lam_init):
    i = pl.program_id(1)
    first = lax.broadcasted_iota(jnp.int32, (tq, LANES), 1) < ATTN_HEAD_DIM
    for h in range(ATTN_HEADS):
        q = q_ref[:, h * LANES:(h + 1) * LANES]
        zero = jnp.zeros_like(q)
        qq_ref[h] = jnp.concatenate([jnp.where(first, q, zero), jnp.where(first, zero, q)], axis=0)

    def block(j, diagonal):
        rows = pl.ds(pl.multiple_of(j * tq, tq), tq)
        if diagonal:
            r = lax.broadcasted_iota(jnp.int32, (2 * tq, tq), 0)
            c = lax.broadcasted_iota(jnp.int32, (2 * tq, tq), 1)
            keep = c <= jnp.where(r >= tq, r - tq, r)
        ones = jnp.ones((tq, LANES), BF16)
        for h in range(ATTN_HEADS):
            sl = slice(h * LANES, (h + 1) * LANES)
            s = _dot_nt(qq_ref[h], kb_ref[rows, sl])
            if diagonal:
                s = jnp.where(keep, s, NEG_INF)
                m_new = jnp.broadcast_to(jnp.max(s, axis=1, keepdims=True), (2 * tq, LANES))
            else:
                m_old = m_ref[h]
                m_new = jnp.maximum(m_old, jnp.max(s, axis=1, keepdims=True))
            p = jnp.exp2(s - jnp.concatenate([m_new] * (tq // LANES), axis=1))
            pv = _dot(p.astype(BF16), jnp.concatenate([vb_ref[rows, sl], ones], axis=1))
            if diagonal:
                l_ref[h] = pv[:, LANES:]
                acc_ref[h] = pv[:, :LANES]
            else:
                alpha = jnp.exp2(m_old - m_new)
                l_ref[h] = alpha * l_ref[h] + pv[:, LANES:]
                acc_ref[h] = alpha * acc_ref[h] + pv[:, :LANES]
            m_ref[h] = m_new

    def body(j, carry):
        block(j, False)
        return carry

    block(i, True)
    lax.fori_loop(0, i, body, 0)

    lam = _lambda_value(lam_ref, lam_init)
    for h in range(ATTN_HEADS):
        o = acc_ref[h] / l_ref[h]
        o_ref[:, h * LANES:(h + 1) * LANES] = _head_out(o[:tq], o[tq:], lam, sub_ref[...], lam_init).astype(BF16)


def _attn_prompt(q, kb, vb, lam_vec, subln, lam_init, batch, seq_len):
    tq = 256 if seq_len % 256 == 0 else seq_len
    nq = seq_len // tq
    q_spec = pl.BlockSpec((tq, D_MODEL), lambda b, i: (b * nq + i, 0))
    kv_spec = pl.BlockSpec((seq_len, D_MODEL), lambda b, i: (b, 0))
    return pl.pallas_call(
        functools.partial(_attn_prompt_kernel, tq=tq, lam_init=lam_init),
        grid=(batch, nq),
        in_specs=[q_spec, kv_spec, kv_spec, _resident((4, ATTN_HEAD_DIM)), _resident((1, ATTN_VDIM))],
        out_specs=q_spec,
        out_shape=jax.ShapeDtypeStruct((batch * seq_len, D_MODEL), BF16),
        scratch_shapes=[pltpu.VMEM((ATTN_HEADS, 2 * tq, LANES), BF16), pltpu.VMEM((ATTN_HEADS, 2 * tq, LANES), F32),
                        pltpu.VMEM((ATTN_HEADS, 2 * tq, LANES), F32), pltpu.VMEM((ATTN_HEADS, 2 * tq, LANES), F32)],
        compiler_params=_cparams(2), name="attn_prompt",
    )(q, kb, vb, lam_vec, subln.reshape(1, ATTN_VDIM))


DECODE_PAGES_PER_STEP = 8


def _attn_decode_kernel(pt_ref, q_ref, kn_ref, vn_ref, *refs, pages_per_step, lam_init):
    del pt_ref
    kc_refs, vc_refs = refs[:pages_per_step], refs[pages_per_step:2 * pages_per_step]
    lam_ref, sub_ref, o_ref, m_ref, l_ref, acc_ref = refs[2 * pages_per_step:]
    step = pl.program_id(1)

    @pl.when(step == 0)
    def _():
        m_ref[...] = jnp.full(m_ref.shape, NEG_INF, F32)
        l_ref[...] = jnp.zeros(l_ref.shape, F32)
        acc_ref[...] = jnp.zeros(acc_ref.shape, F32)

    q = q_ref[...]
    first = lax.broadcasted_iota(jnp.int32, (ATTN_HEADS, LANES), 1) < ATTN_HEAD_DIM
    qsel = jnp.concatenate([jnp.where(first, q, 0.0), jnp.where(first, 0.0, q)], axis=0)
    qb = qsel.astype(BF16)
    rows = kc_refs[0].shape[0] * ATTN_HEADS
    same_head = ((lax.broadcasted_iota(jnp.int32, (2 * ATTN_HEADS, rows), 1) & (ATTN_HEADS - 1))
                 == (lax.broadcasted_iota(jnp.int32, (2 * ATTN_HEADS, rows), 0) & (ATTN_HEADS - 1)))
    s = [jnp.where(same_head, _dot_nt(qb, kc_ref[...].reshape(rows, LANES).astype(BF16)), NEG_INF)
         for kc_ref in kc_refs]
    m_old = m_ref[...]
    m_new = m_old
    for si in s:
        m_new = jnp.maximum(m_new, jnp.max(si, axis=1, keepdims=True))
    alpha = jnp.exp2(m_old - m_new)
    p = [jnp.exp2(si - m_new) for si in s]
    l_new = alpha * l_ref[...]
    acc = alpha * acc_ref[...]
    for pi, vc_ref in zip(p, vc_refs):
        l_new = l_new + jnp.sum(pi, axis=1, keepdims=True)
        acc = acc + _dot(pi.astype(BF16), vc_ref[...].reshape(rows, LANES).astype(BF16))
    m_ref[...] = m_new
    l_ref[...] = l_new
    acc_ref[...] = acc

    @pl.when(step == pl.num_programs(1) - 1)
    def _():
        kn = jnp.concatenate([kn_ref[...]] * 2, axis=0)
        vn = jnp.concatenate([vn_ref[...]] * 2, axis=0)
        sn = jnp.sum(qsel * kn, axis=1, keepdims=True)
        m_fin = jnp.maximum(m_new, sn)
        a_fin = jnp.exp2(m_new - m_fin)
        pn = jnp.exp2(sn - m_fin)
        o = (a_fin * acc + pn * vn) / (a_fin * l_new + pn)
        lam = _lambda_value(lam_ref, lam_init)
        o_ref[...] = _head_out(o[:ATTN_HEADS], o[ATTN_HEADS:], lam, sub_ref[...], lam_init)


def _attn_decode(q, k_new, v_new, cache_k, cache_v, page_table, layer, lam_vec, subln, lam_init):
    batch, n_pages = page_table.shape
    page = cache_k.shape[2]
    pps = DECODE_PAGES_PER_STEP if n_pages % DECODE_PAGES_PER_STEP == 0 else 1
    tok_spec = pl.BlockSpec((None, ATTN_HEADS, LANES), lambda b, s, pt: (b, 0, 0))
    page_specs = [pl.BlockSpec((None, None, page, ATTN_HEADS, LANES),
                               lambda b, s, pt, i=i: (layer, pt[b, s * pps + i], 0, 0, 0)) for i in range(pps)]
    const2 = lambda b, s, pt: (0, 0)
    grid_spec = pltpu.PrefetchScalarGridSpec(
        num_scalar_prefetch=1,
        grid=(batch, n_pages // pps),
        in_specs=[tok_spec, tok_spec, tok_spec] + page_specs + page_specs
                 + [pl.BlockSpec((4, ATTN_HEAD_DIM), const2), pl.BlockSpec((1, ATTN_VDIM), const2)],
        out_specs=tok_spec,
        scratch_shapes=[pltpu.VMEM((2 * ATTN_HEADS, 1), F32), pltpu.VMEM((2 * ATTN_HEADS, 1), F32),
                        pltpu.VMEM((2 * ATTN_HEADS, LANES), F32)],
    )
    return pl.pallas_call(
        functools.partial(_attn_decode_kernel, pages_per_step=pps, lam_init=lam_init),
        grid_spec=grid_spec,
        out_shape=jax.ShapeDtypeStruct((batch, ATTN_HEADS, LANES), F32),
        compiler_params=_cparams(2), name="attn_decode",
    )(page_table, q, k_new, v_new, *([cache_k] * pps), *([cache_v] * pps), lam_vec, subln.reshape(1, ATTN_VDIM))


def _pair_ones():
    r = lax.broadcasted_iota(jnp.int32, (LANES, LANES), 0)
    c = lax.broadcasted_iota(jnp.int32, (LANES, LANES), 1)
    return ((r < RWKV_HEAD) == (c < RWKV_HEAD)).astype(BF16)


def _head_sums(z, ones2):
    return jnp.concatenate(
        [_dot(z[:, c * LANES:(c + 1) * LANES].astype(BF16), ones2) for c in range(D_MODEL // LANES)], axis=1)


def _rwkv_pre_kernel(x_ref, p_ref, g_ref, mix_ref, vec_ref, wr_ref, wk_ref, wv_ref, w1_ref, w2_ref, a1_ref, a2_ref,
                     g1_ref, g2_ref, r_o, lw_o, k_o, v_o, kk_o, b_o, gate_o, bonus_o, *, seq_tiles):
    tm = x_ref.shape[0]
    g = g_ref[...]
    hn = _rms(x_ref[...], g, NORM_EPS)
    if seq_tiles:
        hp = _rms(p_ref[7:8, :], g, NORM_EPS)
        hp = jnp.where(pl.program_id(0) % seq_tiles == 0, 0.0, hp)
        row = lax.broadcasted_iota(jnp.int32, (tm, D_MODEL), 0)
        prev = jnp.where(row == 0, hp, pltpu.roll(hn, 1, 0))
    else:
        prev = p_ref[...]
    xx = prev - hn
    mix = mix_ref[...]
    vec = vec_ref[...]
    w0, a0, k_k, k_a, r_k = (vec[c:c + 1] for c in range(5))

    def mixed(c):
        return (hn + xx * mix[c:c + 1]).astype(BF16)

    r = _dot(mixed(0), wr_ref[...])
    z = w0 + _dot(jnp.tanh(_dot(mixed(1), w1_ref[...])).astype(BF16), w2_ref[...])
    k = _dot(mixed(2), wk_ref[...])
    v = _dot(mixed(3), wv_ref[...])
    a = jax.nn.sigmoid(a0 + _dot(_dot(mixed(4), a1_ref[...]).astype(BF16), a2_ref[...]))
    gate = _dot(jax.nn.sigmoid(_dot(mixed(5), g1_ref[...])).astype(BF16), g2_ref[...])
    softplus = jnp.maximum(-z, 0.0) + jnp.log(1.0 + jnp.exp(-jnp.abs(z)))
    lw = -jnp.exp(-softplus - 0.5)
    ones2 = _pair_ones()
    kk = k * k_k
    kk = kk * lax.rsqrt(jnp.maximum(_head_sums(kk * kk, ones2), 1e-24))
    k = k * (1.0 + (a - 1.0) * k_a)
    r_o[...] = r
    lw_o[...] = lw
    k_o[...] = k
    v_o[...] = v
    kk_o[...] = kk
    b_o[...] = kk * a
    gate_o[...] = gate
    bonus_o[...] = _head_sums(r * k * r_k, ones2) * v


def _rwkv_pre(x, prev, g, mix, vec, mats, seq_len):
    m = x.shape[0]
    if prev is None:
        tm = _row_tile(seq_len, 256)
        seq_tiles = seq_len // tm
        p_arg = x
        p_spec = pl.BlockSpec((8, D_MODEL), lambda i: (jnp.maximum(i * (tm // 8) - 1, 0), 0))
    else:
        tm = _row_tile(m, 256)
        seq_tiles = 0
        p_arg = prev
        p_spec = pl.BlockSpec((tm, D_MODEL), lambda i: (i, 0))
    row_spec = pl.BlockSpec((tm, D_MODEL), lambda i: (i, 0))
    return pl.pallas_call(
        functools.partial(_rwkv_pre_kernel, seq_tiles=seq_tiles),
        grid=(m // tm,),
        in_specs=[row_spec, p_spec, _resident((1, D_MODEL)), _resident(mix.shape), _resident(vec.shape)]
                 + [_resident(w.shape) for w in mats],
        out_specs=[row_spec] * 8,
        out_shape=[jax.ShapeDtypeStruct((m, D_MODEL), F32)] * 8,
        compiler_params=_cparams(1), name="rwkv_pre",
    )(x, p_arg, g.reshape(1, D_MODEL), mix, vec, *mats)


def _wkv_kernel(r_ref, lw_ref, k_ref, v_ref, kk_ref, b_ref, gate_ref, bonus_ref, lnw_ref, lnb_ref,
                y_ref, sout_ref, s_ref, ar_ref, bk_ref, tinv_ref, aab_ref, rbk_ref, x0_ref, pc_ref,
                *, n_chunks):
    c_len = WKV_CHUNK
    n_pairs = RWKV_HEADS // 2
    pairs = range(n_pairs)
    t = pl.program_id(1)

    @pl.when(t == 0)
    def _():
        s_ref[...] = jnp.zeros(s_ref.shape, F32)

    rr = lax.broadcasted_iota(jnp.int32, (c_len, c_len), 0)
    cc = lax.broadcasted_iota(jnp.int32, (c_len, c_len), 1)
    ltri = (rr >= cc).astype(F32)
    r2 = lax.broadcasted_iota(jnp.int32, (LANES, LANES), 0)
    c2 = lax.broadcasted_iota(jnp.int32, (LANES, LANES), 1)
    strict = r2 > c2
    incl = r2 >= c2
    eye = (r2 == c2).astype(F32)
    ones2 = _pair_ones()
    first = lax.broadcasted_iota(jnp.int32, (c_len, LANES), 1) < RWKV_HEAD
    lanes = [slice(p * LANES, (p + 1) * LANES) for p in pairs]

    def stack(x):
        zero = jnp.zeros_like(x)
        return jnp.concatenate([jnp.where(first, x, zero), jnp.where(first, zero, x)], axis=0)

    def prepare(c):
        rows = pl.ds(c * c_len, c_len)
        lw = lw_ref[rows, :]
        cum = jnp.dot(ltri, lw, preferred_element_type=F32, precision=HIGHEST)
        e_in = jnp.exp(cum)
        e_neg = jnp.exp(-cum)
        at_all = (-(kk_ref[rows, :] * jnp.exp(cum - lw))).astype(BF16)
        rt_all = (r_ref[rows, :] * e_in).astype(BF16)
        bt_all = (b_ref[rows, :] * e_neg).astype(BF16)
        kt_all = (k_ref[rows, :] * e_neg).astype(BF16)
        v_all = v_ref[rows, :].astype(BF16)
        ar_ref[c] = jnp.concatenate([at_all, rt_all], axis=0)
        pc_ref[c] = e_in[c_len - 8:c_len, :]
        bk = [jnp.concatenate([stack(bt_all[:, sl]), stack(kt_all[:, sl])], axis=0) for sl in lanes]
        ar = [jnp.concatenate([stack(at_all[:, sl]), stack(rt_all[:, sl])], axis=0) for sl in lanes]
        gram = [_dot_nt(ar[p], bk[p]) for p in pairs]
        vs = [stack(v_all[:, sl]) for sl in lanes]
        a_ab = [jnp.where(strict, gram[p][:LANES, :LANES], 0.0) for p in pairs]
        a_ak = [jnp.where(strict, gram[p][:LANES, LANES:], 0.0).astype(BF16) for p in pairs]
        for p in pairs:
            bk_ref[c, p] = bk[p]
            both_incl = jnp.concatenate([incl, incl], axis=1)
            rbk_ref[c, p] = jnp.where(both_incl, gram[p][LANES:, :], 0.0).astype(BF16)
        x0 = [_dot(a_ak[p], vs[p]) for p in pairs]
        for p in pairs:
            x0_ref[c, p] = x0[p]
        tinv = [eye + a_ab[p] for p in pairs]
        pw = [a_ab[p].astype(BF16) for p in pairs]
        for p in pairs:
            aab_ref[c, p] = pw[p]
        pw = [_dot(pw[p], pw[p]).astype(BF16) for p in pairs]
        for _ in range(int(math.log2(c_len)) - 2):
            both = [_dot(pw[p], jnp.concatenate([tinv[p].astype(BF16), pw[p]], axis=1)) for p in pairs]
            tinv = [tinv[p] + both[p][:, :LANES] for p in pairs]
            pw = [both[p][:, LANES:].astype(BF16) for p in pairs]
        for p in pairs:
            tinv_ref[c, p] = (tinv[p] + _dot(pw[p], tinv[p].astype(BF16))).astype(BF16)

    def advance(c):
        rows = pl.ds(c * c_len, c_len)
        ar = ar_ref[c]
        s = [s_ref[p] for p in pairs]
        m1 = [_dot_nt(ar[:, lanes[p]], s[p].astype(BF16)) for p in pairs]
        x = [stack(m1[p][:c_len]) + x0_ref[c, p] for p in pairs]
        u = [_dot(tinv_ref[c, p], x[p].astype(BF16)) for p in pairs]
        for _ in range(WKV_REFINE_STEPS):
            res = [x[p] - u[p] + _dot(aab_ref[c, p], u[p].astype(BF16)) for p in pairs]
            u = [u[p] + _dot(tinv_ref[c, p], res[p].astype(BF16)) for p in pairs]
        uv = [jnp.concatenate([u[p].astype(BF16), stack(v_ref[rows, lanes[p]].astype(BF16))], axis=0)
              for p in pairs]
        y2 = [stack(m1[p][c_len:]) + _dot(rbk_ref[c, p], uv[p]) for p in pairs]
        upd = [_dot_tn(uv[p], bk_ref[c, p]) for p in pairs]
        decay = pc_ref[c]
        for p in pairs:
            s_ref[p] = (s[p] + upd[p]) * decay[7:8, lanes[p]]
        y = [y2[p][:c_len] + y2[p][c_len:] for p in pairs]
        mu = [_dot(y[p].astype(BF16), ones2) * (1.0 / RWKV_HEAD) for p in pairs]
        d = [y[p] - mu[p] for p in pairs]
        var = [_dot((d[p] * d[p]).astype(BF16), ones2) * (1.0 / RWKV_HEAD) for p in pairs]
        for p in pairs:
            sl = lanes[p]
            yn = d[p] * lax.rsqrt(var[p] + RWKV_GN_EPS) * lnw_ref[:, sl] + lnb_ref[:, sl]
            y_ref[rows, sl] = ((yn + bonus_ref[rows, sl]) * gate_ref[rows, sl]).astype(y_ref.dtype)

    prepare(0)
    for c in range(n_chunks):
        if c + 1 < n_chunks:
            prepare(c + 1)
        advance(c)

    @pl.when(t == pl.num_programs(1) - 1)
    def _():
        sout_ref[...] = s_ref[...]


def _wkv_prompt(streams, ln_w, ln_b, batch, seq_len):
    tw = _row_tile(seq_len, 256)
    nt = seq_len // tw
    nc = tw // WKV_CHUNK
    row_spec = pl.BlockSpec((tw, D_MODEL), lambda b, t: (b * nt + t, 0))
    n_pairs = RWKV_HEADS // 2
    pair_mat = lambda rows, cols, dt: pltpu.VMEM((nc, n_pairs, rows, cols), dt)
    return pl.pallas_call(
        functools.partial(_wkv_kernel, n_chunks=nc),
        grid=(batch, nt),
        in_specs=[row_spec] * 8 + [_resident((1, D_MODEL)), _resident((1, D_MODEL))],
        out_specs=[row_spec, pl.BlockSpec((None, n_pairs, LANES, LANES), lambda b, t: (b, 0, 0, 0))],
        out_shape=[jax.ShapeDtypeStruct((batch * seq_len, D_MODEL), BF16),
                   jax.ShapeDtypeStruct((batch, n_pairs, LANES, LANES), F32)],
        scratch_shapes=[pltpu.VMEM((n_pairs, LANES, LANES), F32),
                        pltpu.VMEM((nc, 2 * WKV_CHUNK, D_MODEL), BF16),
                        pair_mat(4 * WKV_CHUNK, LANES, BF16),
                        pair_mat(LANES, LANES, BF16), pair_mat(LANES, LANES, BF16),
                        pair_mat(LANES, 2 * LANES, BF16),
                        pair_mat(LANES, LANES, F32),
                        pltpu.VMEM((nc, 8, D_MODEL), F32)],
        compiler_params=_cparams(2), name="wkv_prompt",
    )(*streams, ln_w.reshape(1, D_MODEL), ln_b.reshape(1, D_MODEL))


def _wkv_step_kernel(s_ref, r_ref, lw_ref, k_ref, v_ref, kk_ref, b_ref, gate_ref, bonus_ref, lnw_ref, lnb_ref,
                     y_ref, sout_ref):
    n = RWKV_HEAD
    s = s_ref[...]
    eye = (lax.broadcasted_iota(jnp.int32, (n, n), 0) == lax.broadcasted_iota(jnp.int32, (n, n), 1)).astype(F32)
    sa = -jnp.sum(s * kk_ref[...], axis=-1, keepdims=True)
    vcol = jnp.sum(eye * v_ref[...], axis=-1, keepdims=True)
    s = s * jnp.exp(lw_ref[...]) + sa * b_ref[...] + vcol * k_ref[...]
    sout_ref[...] = s
    ycol = jnp.sum(s * r_ref[...], axis=-1, keepdims=True)
    y = jnp.sum(eye * ycol, axis=-2, keepdims=True)
    mu = jnp.mean(y, axis=-1, keepdims=True)
    d = y - mu
    var = jnp.mean(d * d, axis=-1, keepdims=True)
    yn = d * lax.rsqrt(var + RWKV_GN_EPS) * lnw_ref[...] + lnb_ref[...]
    y_ref[...] = (yn + bonus_ref[...]) * gate_ref[...]


def _wkv_step(state, streams, ln_w, ln_b):
    batch = state.shape[0]
    nb = 8 if batch % 8 == 0 else batch
    vec = lambda z: z.reshape(batch, RWKV_HEADS, 1, RWKV_HEAD)
    st_spec = pl.BlockSpec((nb, RWKV_HEADS, RWKV_HEAD, RWKV_HEAD), lambda i: (i, 0, 0, 0))
    vec_spec = pl.BlockSpec((nb, RWKV_HEADS, 1, RWKV_HEAD), lambda i: (i, 0, 0, 0))
    ln_spec = pl.BlockSpec((1, RWKV_HEADS, 1, RWKV_HEAD), lambda i: (0, 0, 0, 0))
    y, s_new = pl.pallas_call(
        _wkv_step_kernel,
        grid=(batch // nb,),
        in_specs=[st_spec] + [vec_spec] * 8 + [ln_spec, ln_spec],
        out_specs=[vec_spec, st_spec],
        out_shape=[jax.ShapeDtypeStruct((batch, RWKV_HEADS, 1, RWKV_HEAD), F32),
                   jax.ShapeDtypeStruct(state.shape, F32)],
        compiler_params=_cparams(1), name="wkv_step",
    )(state, *[vec(z) for z in streams], ln_w.reshape(1, RWKV_HEADS, 1, RWKV_HEAD),
      ln_b.reshape(1, RWKV_HEADS, 1, RWKV_HEAD))
    return y.reshape(batch, D_MODEL), s_new


CONV_HALO = 32
CONV_TAP_ROWS = 64


def _conv_glu_kernel(x_ref, g_ref, w_ref, b_ref, u_ref):
    h = _rms(x_ref[...], g_ref[...], NORM_EPS).astype(BF16)
    b = b_ref[...]
    a = _dot(h, w_ref[:, :D_MODEL]) + b[:, :D_MODEL]
    gate = _dot(h, w_ref[:, D_MODEL:]) + b[:, D_MODEL:]
    u_ref[...] = a * jax.nn.sigmoid(gate)


def _conv_glu(x, g, w1, b1):
    m = x.shape[0]
    tm = _row_tile(m, 512)
    row_spec = pl.BlockSpec((tm, D_MODEL), lambda i: (i, 0))
    return pl.pallas_call(
        _conv_glu_kernel,
        grid=(m // tm,),
        in_specs=[row_spec, _resident((1, D_MODEL)), _resident((D_MODEL, 2 * D_MODEL)), _resident((1, 2 * D_MODEL))],
        out_specs=row_spec,
        out_shape=jax.ShapeDtypeStruct((m, D_MODEL), F32),
        compiler_params=_cparams(1), name="conv_glu",
    )(x, g.reshape(1, D_MODEL), w1, b1.reshape(1, 2 * D_MODEL))


def _ln_silu(y, ln_w, ln_b):
    mu = jnp.mean(y, axis=-1, keepdims=True)
    d = y - mu
    var = jnp.mean(d * d, axis=-1, keepdims=True)
    yn = d * lax.rsqrt(var + LN_EPS) * ln_w + ln_b
    return yn * jax.nn.sigmoid(yn)


def _conv_dw_kernel(u_ref, halo_ref, dw_ref, vec_ref, w2_ref, res_ref, o_ref, ext_ref, s_ref, *, seq_tiles):
    tm = u_ref.shape[0]
    ext_ref[0:CONV_HALO, :] = jnp.where(pl.program_id(0) % seq_tiles == 0, 0.0, halo_ref[...])
    ext_ref[CONV_HALO:CONV_HALO + tm, :] = u_ref[...]
    vec = vec_ref[...]
    dw_b, ln_w, ln_b, b2 = (vec[c:c + 1] for c in range(4))
    dw = dw_ref[...]
    phases = [[] for _ in range(8)]
    for k in range(CONV_WIDTH):
        off = CONV_HALO - CONV_BUF + k
        phases[off % 8].append((k, off - off % 8))
    for c in range(tm // CONV_TAP_ROWS):
        base = c * CONV_TAP_ROWS
        acc = dw_b
        for b, taps in enumerate(phases):
            part = None
            n_rows = CONV_TAP_ROWS + (8 if b else 0)
            for k, off in taps:
                term = dw[k:k + 1] * ext_ref[base + off:base + off + n_rows, :]
                part = term if part is None else part + term
            acc = acc + part[b:b + CONV_TAP_ROWS]
        s_ref[base:base + CONV_TAP_ROWS, :] = _ln_silu(acc, ln_w, ln_b).astype(BF16)
    o_ref[...] = res_ref[...] + _dot(s_ref[...], w2_ref[...]) + b2


def _conv_dw(u, dw, vec, w2, res, seq_len):
    m = u.shape[0]
    tm = _row_tile(seq_len, 256)
    per = tm // CONV_HALO
    row_spec = pl.BlockSpec((tm, D_MODEL), lambda i: (i, 0))
    halo_spec = pl.BlockSpec((CONV_HALO, D_MODEL), lambda i: (jnp.maximum(i * per - 1, 0), 0))
    return pl.pallas_call(
        functools.partial(_conv_dw_kernel, seq_tiles=seq_len // tm),
        grid=(m // tm,),
        in_specs=[row_spec, halo_spec, _resident(dw.shape), _resident(vec.shape), _resident((D_MODEL, D_MODEL)), row_spec],
        out_specs=row_spec,
        out_shape=jax.ShapeDtypeStruct((m, D_MODEL), F32),
        scratch_shapes=[pltpu.VMEM((CONV_HALO + tm, D_MODEL), F32), pltpu.VMEM((tm, D_MODEL), BF16)],
        compiler_params=_cparams(1), name="conv_dw",
    )(u, u, dw, vec, w2, res)


def _conv_step_kernel(buf_ref, u_ref, dw_ref, vec_ref, s_ref):
    vec = vec_ref[...]
    dw = dw_ref[...]
    y = (jnp.sum(buf_ref[...] * dw[0:CONV_BUF], axis=1, keepdims=True)
         + u_ref[...] * dw[CONV_BUF:CONV_WIDTH] + vec[0:1])
    s_ref[...] = _ln_silu(y, vec[1:2], vec[2:3])


def _conv_step(buf, u, dw, vec):
    batch = buf.shape[0]
    nb = 8 if batch % 8 == 0 else batch
    one_spec = pl.BlockSpec((nb, 1, D_MODEL), lambda i: (i, 0, 0))
    return pl.pallas_call(
        _conv_step_kernel,
        grid=(batch // nb,),
        in_specs=[pl.BlockSpec((nb, CONV_BUF, D_MODEL), lambda i: (i, 0, 0)), one_spec,
                  _resident(dw.shape), _resident(vec.shape)],
        out_specs=one_spec,
        out_shape=jax.ShapeDtypeStruct((batch, 1, D_MODEL), F32),
        compiler_params=_cparams(1), name="conv_step",
    )(buf, u.reshape(batch, 1, D_MODEL), dw, vec).reshape(batch, D_MODEL)


def _rows8(*rows):
    pad = [jnp.zeros((D_MODEL,), F32)] * (8 - len(rows))
    return jnp.stack([r.reshape(D_MODEL).astype(F32) for r in rows] + pad)


def kernel(x_prompt, x_sample, cache_k, cache_v, page_table, state_wkv, state_shift, state_conv, norm_mix, norm_ffn, norm_final, attn_w_qkv, attn_w_o, attn_lambda, attn_subln, rwkv_mix, rwkv_w0, rwkv_w1, rwkv_w2, rwkv_a0, rwkv_a1, rwkv_a2, rwkv_g1, rwkv_g2, rwkv_k_k, rwkv_k_a, rwkv_r_k, rwkv_wr, rwkv_wk, rwkv_wv, rwkv_wo, rwkv_ln_w, rwkv_ln_b, conv_w1, conv_b1, conv_dw, conv_dw_b, conv_ln_w, conv_ln_b, conv_w2, conv_b2, ffn_w_up, ffn_w_down):
    bp, seq, d = x_prompt.shape
    bs = x_sample.shape[0]
    n_layers_attn, n_pool, page, heads, vdim = cache_k.shape
    past_len = page_table.shape[1] * page
    xp = x_prompt.reshape(bp * seq, d)
    xs = x_sample.reshape(bs, d)
    tabs_p = _rope_tables(jnp.arange(seq, dtype=jnp.int32))
    tabs_s = _rope_tables(jnp.full((bs,), past_len, jnp.int32))
    no_bias = jnp.zeros((d,), F32)
    kv_p = kv_s = n_layers_attn
    wkv_p, wkv_s, sh_p, sh_s, cv_p, cv_s = ([] for _ in range(6))
    for i in range(DEPTH):
        j = i // N_MIXERS
        kind = i % N_MIXERS
        if kind == 0:
            lam_init = 0.8 - 0.6 * math.exp(-0.3 * i)
            wqkv = attn_w_qkv[j].astype(BF16)
            wo = attn_w_o[j].astype(BF16)
            q, kb, vb, kv_p = _attn_qkv(xp, norm_mix[i], wqkv, tabs_p, seq, j, kv_p)
            o = _attn_prompt(q, kb, vb, attn_lambda[j], attn_subln[j], lam_init, bp, seq)
            mix_p = (o, wo, no_bias)
            q_s, _, _, kv_s = _attn_qkv(xs, norm_mix[i], wqkv, tabs_s, bs, j, kv_s)
            tok = lambda z: z.reshape(bs, heads, vdim)
            o_s = _attn_decode(tok(q_s.astype(F32)), tok(kv_s[0][j]), tok(kv_s[1][j]), cache_k, cache_v, page_table,
                               j, attn_lambda[j], attn_subln[j], lam_init)
            mix_s = (o_s.reshape(bs, d), wo, no_bias)
        elif kind == 1:
            mats = [w[j].astype(BF16) for w in (rwkv_wr, rwkv_wk, rwkv_wv, rwkv_w1, rwkv_w2, rwkv_a1, rwkv_a2,
                                                rwkv_g1, rwkv_g2)]
            vec = _rows8(rwkv_w0[j], rwkv_a0[j], rwkv_k_k[j], rwkv_k_a[j], rwkv_r_k[j])
            wo = rwkv_wo[j].astype(BF16)
            sh_p.append(_norm_rows(xp.reshape(bp, seq, d)[:, -1], norm_mix[i]))
            sh_s.append(_norm_rows(xs, norm_mix[i]))
            streams = _rwkv_pre(xp, None, norm_mix[i], rwkv_mix[j], vec, mats, seq)
            y, s_pairs = _wkv_prompt(streams, rwkv_ln_w[j], rwkv_ln_b[j], bp, seq)
            mix_p = (y, wo, no_bias)
            s6 = s_pairs.reshape(bp, RWKV_HEADS // 2, 2, RWKV_HEAD, 2, RWKV_HEAD)
            wkv_p.append(jnp.stack([s6[:, :, 0, :, 0, :], s6[:, :, 1, :, 1, :]], axis=2)
                         .reshape(bp, RWKV_HEADS, RWKV_HEAD, RWKV_HEAD))
            streams = _rwkv_pre(xs, state_shift[j], norm_mix[i], rwkv_mix[j], vec, mats, 1)
            y_s, s_new = _wkv_step(state_wkv[j], streams, rwkv_ln_w[j], rwkv_ln_b[j])
            mix_s = (y_s, wo, no_bias)
            wkv_s.append(s_new)
        else:
            w1 = conv_w1[j].astype(BF16)
            w2 = conv_w2[j].astype(BF16)
            dw = jnp.concatenate([conv_dw[j], jnp.zeros((1, d), F32)], axis=0)
            vec = _rows8(conv_dw_b[j], conv_ln_w[j], conv_ln_b[j], conv_b2[j])
            u = _conv_glu(xp, norm_mix[i], w1, conv_b1[j])
            xp = _conv_dw(u, dw, vec, w2, xp, seq)
            cv_p.append(u.reshape(bp, seq, d)[:, seq - CONV_BUF:])
            u_s = _conv_glu(xs, norm_mix[i], w1, conv_b1[j])
            mix_p = None
            mix_s = (_conv_step(state_conv[j], u_s, dw, vec), w2, conv_b2[j])
            cv_s.append(jnp.concatenate([state_conv[j][:, 1:], u_s[:, None, :]], axis=1))
        wup = ffn_w_up[i].astype(BF16)
        wdn = ffn_w_down[i].astype(BF16)
        out_norm = norm_final if i == DEPTH - 1 else None
        xp = _ffn(xp, norm_ffn[i], wup, wdn, mix_p, out_norm)
        xs = _ffn(xs, norm_ffn[i], wup, wdn, mix_s, out_norm)
    return (xp.reshape(bp, seq, d), xs.reshape(bs, 1, d),
            kv_p[0].reshape(n_layers_attn, bp, seq, heads, vdim), kv_p[1].reshape(n_layers_attn, bp, seq, heads, vdim),
            kv_s[0].reshape(n_layers_attn, bs, 1, heads, vdim), kv_s[1].reshape(n_layers_attn, bs, 1, heads, vdim),
            jnp.stack(wkv_p), jnp.stack(wkv_s),
            jnp.stack(sh_p), jnp.stack(sh_s), jnp.stack(cv_p), jnp.stack(cv_s))
```

```python
import functools
import math

import jax
import jax.numpy as jnp
from jax import lax
from jax.experimental import pallas as pl
from jax.experimental.pallas import tpu as pltpu

D_MODEL = 1024
DEPTH = 4
N_MIXERS = 3
ATTN_HEAD_DIM = 64
ATTN_HEADS = 8
ATTN_VDIM = 128
ROPE_THETA = 10000.0
NEG_INF = -1e30
RWKV_HEAD = 64
RWKV_HEADS = 16
RWKV_GN_EPS = 64e-5
CONV_WIDTH = 31
CONV_BUF = CONV_WIDTH - 1
LN_EPS = 1e-5
FFN_HIDDEN = 4 * D_MODEL
NORM_EPS = 1e-6

LANES = 128
VMEM_LIMIT = 56 * 1024 * 1024
WKV_CHUNK = 64
WKV_REFINE_STEPS = 1
Q_SCALE = ATTN_HEAD_DIM ** -0.5 * math.log2(math.e)

F32 = jnp.float32
BF16 = jnp.bfloat16
HIGHEST = lax.Precision.HIGHEST


def _cparams(n_axes):
    return pltpu.CompilerParams(dimension_semantics=("arbitrary",) * n_axes, vmem_limit_bytes=VMEM_LIMIT)


def _resident(shape):
    nd = len(shape)
    return pl.BlockSpec(shape, lambda *_: (0,) * nd, pipeline_mode=pl.Buffered(1))


def _dot(a, b):
    return jnp.dot(a, b, preferred_element_type=F32)


def _dot_nt(a, b, precision=None):
    return lax.dot_general(a, b, (((1,), (1,)), ((), ())), preferred_element_type=F32, precision=precision)


def _dot_tn(a, b, precision=None):
    return lax.dot_general(a, b, (((0,), (0,)), ((), ())), preferred_element_type=F32, precision=precision)


def _rms(x, g, eps):
    return x * lax.rsqrt(jnp.mean(x * x, axis=-1, keepdims=True) + eps) * g


def _row_tile(m, want):
    return want if m % want == 0 else m


def _ffn_kernel(*refs, chunk, with_mixer, with_out_norm):
    if with_out_norm:
        *refs, gout_ref, o_ref = refs
        refs.append(o_ref)
    if with_mixer:
        a_ref, wo_ref, b_ref, x_ref, g_ref, wup_ref, wdn_ref, o_ref = refs
        x = x_ref[...] + _dot(a_ref[...].astype(BF16), wo_ref[...]) + b_ref[...]
    else:
        x_ref, g_ref, wup_ref, wdn_ref, o_ref = refs
        x = x_ref[...]
    h = _rms(x, g_ref[...], NORM_EPS).astype(BF16)
    acc = x
    for c in range(FFN_HIDDEN // chunk):
        u = _dot(h, wup_ref[:, c * chunk:(c + 1) * chunk])
        u = jnp.maximum(u, 0.0)
        acc = acc + _dot((u * u).astype(BF16), wdn_ref[c * chunk:(c + 1) * chunk, :])
    o_ref[...] = _rms(acc, gout_ref[...], NORM_EPS) if with_out_norm else acc


def _ffn(x, g, wup, wdn, mixer=None, out_norm=None):
    m = x.shape[0]
    tm = _row_tile(m, 512)
    row_spec = pl.BlockSpec((tm, D_MODEL), lambda i: (i, 0))
    in_specs = [row_spec, _resident((1, D_MODEL)), _resident((D_MODEL, FFN_HIDDEN)), _resident((FFN_HIDDEN, D_MODEL))]
    args = [x, g.reshape(1, D_MODEL), wup, wdn]
    if mixer is not None:
        a, wo, bias = mixer
        in_specs = [pl.BlockSpec((tm, a.shape[1]), lambda i: (i, 0)), _resident(wo.shape), _resident((1, D_MODEL))] + in_specs
        args = [a, wo, bias.reshape(1, D_MODEL)] + args
    if out_norm is not None:
        in_specs.append(_resident((1, D_MODEL)))
        args.append(out_norm.reshape(1, D_MODEL))
    return pl.pallas_call(
        functools.partial(_ffn_kernel, chunk=1024, with_mixer=mixer is not None, with_out_norm=out_norm is not None),
        grid=(m // tm,),
        in_specs=in_specs,
        out_specs=row_spec,
        out_shape=jax.ShapeDtypeStruct((m, D_MODEL), F32),
        compiler_params=_cparams(1), name="ffn",
    )(*args)


def _norm_kernel(x_ref, g_ref, o_ref):
    o_ref[...] = _rms(x_ref[...], g_ref[...], NORM_EPS)


def _norm_rows(x, g):
    m = x.shape[0]
    tm = _row_tile(m, 1024)
    return pl.pallas_call(
        _norm_kernel,
        grid=(m // tm,),
        in_specs=[pl.BlockSpec((tm, D_MODEL), lambda i: (i, 0)), _resident((1, D_MODEL))],
        out_specs=pl.BlockSpec((tm, D_MODEL), lambda i: (i, 0)),
        out_shape=jax.ShapeDtypeStruct((m, D_MODEL), F32),
        compiler_params=_cparams(1), name="norm_rows",
    )(x, g.reshape(1, D_MODEL))


def _rope_tables(pos):
    half = ATTN_HEAD_DIM // 2
    inv = ROPE_THETA ** (-jnp.arange(half, dtype=F32) / half)
    ang = pos.astype(F32)[:, None] * inv[None, :]
    cos, sin = jnp.cos(ang), jnp.sin(ang)
    zero = jnp.zeros_like(sin)
    cos_t = jnp.tile(cos, (1, 4))
    sin_lo = jnp.tile(jnp.concatenate([-sin, zero], axis=1), (1, 2))
    sin_hi = jnp.tile(jnp.concatenate([zero, sin], axis=1), (1, 2))
    return cos_t, sin_lo, sin_hi


def _qkv_kernel(x_ref, g_ref, w_ref, cos_ref, slo_ref, shi_ref, q_ref, k_ref, v_ref, kb_ref, vb_ref):
    h = _rms(x_ref[...], g_ref[...], NORM_EPS).astype(BF16)
    cos, slo, shi = cos_ref[...], slo_ref[...], shi_ref[...]

    def rope(z):
        return z * cos + pltpu.roll(z, LANES - 32, 1) * slo + pltpu.roll(z, 32, 1) * shi

    q = _dot(h, w_ref[:, 0:D_MODEL])
    for c in range(ATTN_HEADS):
        sl = slice(c * LANES, (c + 1) * LANES)
        q_ref[:, sl] = (rope(q[:, sl]) * Q_SCALE).astype(BF16)
    tm = x_ref.shape[0]
    k = _dot(h, w_ref[:, D_MODEL:2 * D_MODEL])
    for c in range(ATTN_HEADS):
        sl = slice(c * LANES, (c + 1) * LANES)
        kr = rope(k[:, sl])
        k_ref[pl.ds(c, tm, stride=ATTN_HEADS), :] = kr
        kb_ref[:, sl] = kr.astype(BF16)
    v = _dot(h, w_ref[:, 2 * D_MODEL:3 * D_MODEL])
    for c in range(ATTN_HEADS):
        v_ref[pl.ds(c, tm, stride=ATTN_HEADS), :] = v[:, c * LANES:(c + 1) * LANES]
    vb_ref[...] = v.astype(BF16)


def _qkv_kernel_into(x_ref, g_ref, w_ref, cos_ref, slo_ref, shi_ref, k_all_ref, v_all_ref, *out_refs):
    del k_all_ref, v_all_ref
    _qkv_kernel(x_ref, g_ref, w_ref, cos_ref, slo_ref, shi_ref, *out_refs)


def _attn_qkv(x, g, wqkv, tables, seq_len, layer, kv_all):
    m = x.shape[0]
    tm = _row_tile(min(m, seq_len), 512)
    nt = seq_len // tm
    tab_spec = pl.BlockSpec((tm, LANES), lambda i: (i % nt, 0))
    row_spec = pl.BlockSpec((tm, D_MODEL), lambda i: (i, 0))
    slab_spec = pl.BlockSpec((None, tm * ATTN_HEADS, LANES), lambda i: (layer, i, 0))
    in_specs = [row_spec, _resident((1, D_MODEL)), _resident((D_MODEL, 3 * D_MODEL)), tab_spec, tab_spec, tab_spec]
    args = [x, g.reshape(1, D_MODEL), wqkv, *tables]
    if isinstance(kv_all, int):
        n_layers, body, aliases = kv_all, _qkv_kernel, {}
    else:
        n_layers, body, aliases = kv_all[0].shape[0], _qkv_kernel_into, {len(args): 1, len(args) + 1: 2}
        in_specs += [pl.BlockSpec(memory_space=pl.ANY)] * 2
        args += list(kv_all)
    stacked = jax.ShapeDtypeStruct((n_layers, m * ATTN_HEADS, LANES), F32)
    q, k_all, v_all, kb, vb = pl.pallas_call(
        body,
        grid=(m // tm,),
        in_specs=in_specs,
        out_specs=[row_spec, slab_spec, slab_spec, row_spec, row_spec],
        out_shape=[jax.ShapeDtypeStruct((m, D_MODEL), BF16), stacked, stacked,
                   jax.ShapeDtypeStruct((m, D_MODEL), BF16), jax.ShapeDtypeStruct((m, D_MODEL), BF16)],
        input_output_aliases=aliases,
        compiler_params=_cparams(1), name="attn_qkv",
    )(*args)
    return q, kb, vb, (k_all, v_all)


def _lambda_value(lam_ref, lam_init):
    lv = lam_ref[...]
    a = jnp.sum(lv[0:1] * lv[1:2], axis=1, keepdims=True)
    b = jnp.sum(lv[2:3] * lv[3:4], axis=1, keepdims=True)
    return jnp.exp(a) - jnp.exp(b) + lam_init


def _head_out(o1, o2, lam, subln, lam_init):
    o = o1 - lam * o2
    o = o * lax.rsqrt(jnp.mean(o * o, axis=-1, keepdims=True) + LN_EPS)
    return o * subln * (1.0 - lam_init)


def _attn_prompt_kernel(q_ref, kb_ref, vb_ref, lam_ref, sub_ref, o_ref, qq_ref, m_ref, l_ref, acc_ref,
                        *, tq, lam_init):
    i = pl.program_id(1)
    first = lax.broadcasted_iota(jnp.int32, (tq, LANES), 1) < ATTN_HEAD_DIM
    for h in range(ATTN_HEADS):
        q = q_ref[:, h * LANES:(h + 1) * LANES]
        zero = jnp.zeros_like(q)
        qq_ref[h] = jnp.concatenate([jnp.where(first, q, zero), jnp.where(first, zero, q)], axis=0)

    def block(j, diagonal):
        rows = pl.ds(pl.multiple_of(j * tq, tq), tq)
        if diagonal:
            r = lax.broadcasted_iota(jnp.int32, (2 * tq, tq), 0)
            c = lax.broadcasted_iota(jnp.int32, (2 * tq, tq), 1)
            keep = c <= jnp.where(r >= tq, r - tq, r)
        ones = jnp.ones((tq, LANES), BF16)
        for h in range(ATTN_HEADS):
            sl = slice(h * LANES, (h + 1) * LANES)
            s = _dot_nt(qq_ref[h], kb_ref[rows, sl])
            if diagonal:
                s = jnp.where(keep, s, NEG_INF)
                m_new = jnp.broadcast_to(jnp.max(s, axis=1, keepdims=True), (2 * tq, LANES))
            else:
                m_old = m_ref[h]
                m_new = jnp.maximum(m_old, jnp.max(s, axis=1, keepdims=True))
            p = jnp.exp2(s - jnp.concatenate([m_new] * (tq // LANES), axis=1))
            pv = _dot(p.astype(BF16), jnp.concatenate([vb_ref[rows, sl], ones], axis=1))
            if diagonal:
                l_ref[h] = pv[:, LANES:]
                acc_ref[h] = pv[:, :LANES]
            else:
                alpha = jnp.exp2(m_old - m_new)
                l_ref[h] = alpha * l_ref[h] + pv[:, LANES:]
                acc_ref[h] = alpha * acc_ref[h] + pv[:, :LANES]
            m_ref[h] = m_new

    def body(j, carry):
        block(j, False)
        return carry

    block(i, True)
    lax.fori_loop(0, i, body, 0)

    lam = _lambda_value(lam_ref, lam_init)
    for h in range(ATTN_HEADS):
        o = acc_ref[h] / l_ref[h]
        o_ref[:, h * LANES:(h + 1) * LANES] = _head_out(o[:tq], o[tq:], lam, sub_ref[...], lam_init).astype(BF16)


def _attn_prompt(q, kb, vb, lam_vec, subln, lam_init, batch, seq_len):
    tq = 256 if seq_len % 256 == 0 else seq_len
    nq = seq_len // tq
    q_spec = pl.BlockSpec((tq, D_MODEL), lambda b, i: (b * nq + i, 0))
    kv_spec = pl.BlockSpec((seq_len, D_MODEL), lambda b, i: (b, 0))
    return pl.pallas_call(
        functools.partial(_attn_prompt_kernel, tq=tq, lam_init=lam_init),
        grid=(batch, nq),
        in_specs=[q_spec, kv_spec, kv_spec, _resident((4, ATTN_HEAD_DIM)), _resident((1, ATTN_VDIM))],
        out_specs=q_spec,
        out_shape=jax.ShapeDtypeStruct((batch * seq_len, D_MODEL), BF16),
        scratch_shapes=[pltpu.VMEM((ATTN_HEADS, 2 * tq, LANES), BF16), pltpu.VMEM((ATTN_HEADS, 2 * tq, LANES), F32),
                        pltpu.VMEM((ATTN_HEADS, 2 * tq, LANES), F32), pltpu.VMEM((ATTN_HEADS, 2 * tq, LANES), F32)],
        compiler_params=_cparams(2), name="attn_prompt",
    )(q, kb, vb, lam_vec, subln.reshape(1, ATTN_VDIM))


DECODE_PAGES_PER_STEP = 8


def _attn_decode_kernel(pt_ref, q_ref, kn_ref, vn_ref, *refs, pages_per_step, lam_init):
    del pt_ref
    kc_refs, vc_refs = refs[:pages_per_step], refs[pages_per_step:2 * pages_per_step]
    lam_ref, sub_ref, o_ref, m_ref, l_ref, acc_ref = refs[2 * pages_per_step:]
    step = pl.program_id(1)

    @pl.when(step == 0)
    def _():
        m_ref[...] = jnp.full(m_ref.shape, NEG_INF, F32)
        l_ref[...] = jnp.zeros(l_ref.shape, F32)
        acc_ref[...] = jnp.zeros(acc_ref.shape, F32)

    q = q_ref[...]
    first = lax.broadcasted_iota(jnp.int32, (ATTN_HEADS, LANES), 1) < ATTN_HEAD_DIM
    qsel = jnp.concatenate([jnp.where(first, q, 0.0), jnp.where(first, 0.0, q)], axis=0)
    qb = qsel.astype(BF16)
    rows = kc_refs[0].shape[0] * ATTN_HEADS
    same_head = ((lax.broadcasted_iota(jnp.int32, (2 * ATTN_HEADS, rows), 1) & (ATTN_HEADS - 1))
                 == (lax.broadcasted_iota(jnp.int32, (2 * ATTN_HEADS, rows), 0) & (ATTN_HEADS - 1)))
    s = [jnp.where(same_head, _dot_nt(qb, kc_ref[...].reshape(rows, LANES).astype(BF16)), NEG_INF)
         for kc_ref in kc_refs]
    m_old = m_ref[...]
    m_new = m_old
    for si in s:
        m_new = jnp.maximum(m_new, jnp.max(si, axis=1, keepdims=True))
    alpha = jnp.exp2(m_old - m_new)
    p = [jnp.exp2(si - m_new) for si in s]
    l_new = alpha * l_ref[...]
    acc = alpha * acc_ref[...]
    for pi, vc_ref in zip(p, vc_refs):
        l_new = l_new + jnp.sum(pi, axis=1, keepdims=True)
        acc = acc + _dot(pi.astype(BF16), vc_ref[...].reshape(rows, LANES).astype(BF16))
    m_ref[...] = m_new
    l_ref[...] = l_new
    acc_ref[...] = acc

    @pl.when(step == pl.num_programs(1) - 1)
    def _():
        kn = jnp.concatenate([kn_ref[...]] * 2, axis=0)
        vn = jnp.concatenate([vn_ref[...]] * 2, axis=0)
        sn = jnp.sum(qsel * kn, axis=1, keepdims=True)
        m_fin = jnp.maximum(m_new, sn)
        a_fin = jnp.exp2(m_new - m_fin)
        pn = jnp.exp2(sn - m_fin)
        o = (a_fin * acc + pn * vn) / (a_fin * l_new + pn)
        lam = _lambda_value(lam_ref, lam_init)
        o_ref[...] = _head_out(o[:ATTN_HEADS], o[ATTN_HEADS:], lam, sub_ref[...], lam_init)


def _attn_decode(q, k_new, v_new, cache_k, cache_v, page_table, layer, lam_vec, subln, lam_init):
    batch, n_pages = page_table.shape
    page = cache_k.shape[2]
    pps = DECODE_PAGES_PER_STEP if n_pages % DECODE_PAGES_PER_STEP == 0 else 1
    tok_spec = pl.BlockSpec((None, ATTN_HEADS, LANES), lambda b, s, pt: (b, 0, 0))
    page_specs = [pl.BlockSpec((None, None, page, ATTN_HEADS, LANES),
                               lambda b, s, pt, i=i: (layer, pt[b, s * pps + i], 0, 0, 0)) for i in range(pps)]
    const2 = lambda b, s, pt: (0, 0)
    grid_spec = pltpu.PrefetchScalarGridSpec(
        num_scalar_prefetch=1,
        grid=(batch, n_pages // pps),
        in_specs=[tok_spec, tok_spec, tok_spec] + page_specs + page_specs
                 + [pl.BlockSpec((4, ATTN_HEAD_DIM), const2), pl.BlockSpec((1, ATTN_VDIM), const2)],
        out_specs=tok_spec,
        scratch_shapes=[pltpu.VMEM((2 * ATTN_HEADS, 1), F32), pltpu.VMEM((2 * ATTN_HEADS, 1), F32),
                        pltpu.VMEM((2 * ATTN_HEADS, LANES), F32)],
    )
    return pl.pallas_call(
        functools.partial(_attn_decode_kernel, pages_per_step=pps, lam_init=lam_init),
        grid_spec=grid_spec,
        out_shape=jax.ShapeDtypeStruct((batch, ATTN_HEADS, LANES), F32),
        compiler_params=_cparams(2), name="attn_decode",
    )(page_table, q, k_new, v_new, *([cache_k] * pps), *([cache_v] * pps), lam_vec, subln.reshape(1, ATTN_VDIM))


def _pair_ones():
    r = lax.broadcasted_iota(jnp.int32, (LANES, LANES), 0)
    c = lax.broadcasted_iota(jnp.int32, (LANES, LANES), 1)
    return ((r < RWKV_HEAD) == (c < RWKV_HEAD)).astype(BF16)


def _head_sums(z, ones2):
    return jnp.concatenate(
        [_dot(z[:, c * LANES:(c + 1) * LANES].astype(BF16), ones2) for c in range(D_MODEL // LANES)], axis=1)


def _rwkv_pre_kernel(x_ref, p_ref, g_ref, mix_ref, vec_ref, wr_ref, wk_ref, wv_ref, w1_ref, w2_ref, a1_ref, a2_ref,
                     g1_ref, g2_ref, r_o, lw_o, k_o, v_o, kk_o, b_o, gate_o, bonus_o, *, seq_tiles):
    tm = x_ref.shape[0]
    g = g_ref[...]
    hn = _rms(x_ref[...], g, NORM_EPS)
    if seq_tiles:
        hp = _rms(p_ref[7:8, :], g, NORM_EPS)
        hp = jnp.where(pl.program_id(0) % seq_tiles == 0, 0.0, hp)
        row = lax.broadcasted_iota(jnp.int32, (tm, D_MODEL), 0)
        prev = jnp.where(row == 0, hp, pltpu.roll(hn, 1, 0))
    else:
        prev = p_ref[...]
    xx = prev - hn
    mix = mix_ref[...]
    vec = vec_ref[...]
    w0, a0, k_k, k_a, r_k = (vec[c:c + 1] for c in range(5))

    def mixed(c):
        return (hn + xx * mix[c:c + 1]).astype(BF16)

    r = _dot(mixed(0), wr_ref[...])
    z = w0 + _dot(jnp.tanh(_dot(mixed(1), w1_ref[...])).astype(BF16), w2_ref[...])
    k = _dot(mixed(2), wk_ref[...])
    v = _dot(mixed(3), wv_ref[...])
    a = jax.nn.sigmoid(a0 + _dot(_dot(mixed(4), a1_ref[...]).astype(BF16), a2_ref[...]))
    gate = _dot(jax.nn.sigmoid(_dot(mixed(5), g1_ref[...])).astype(BF16), g2_ref[...])
    softplus = jnp.maximum(-z, 0.0) + jnp.log(1.0 + jnp.exp(-jnp.abs(z)))
    lw = -jnp.exp(-softplus - 0.5)
    ones2 = _pair_ones()
    kk = k * k_k
    kk = kk * lax.rsqrt(jnp.maximum(_head_sums(kk * kk, ones2), 1e-24))
    k = k * (1.0 + (a - 1.0) * k_a)
    lw_o[...] = lw
    for ref, val in ((r_o, r), (k_o, k), (v_o, v), (kk_o, kk), (b_o, kk * a), (gate_o, gate),
                     (bonus_o, _head_sums(r * k * r_k, ones2) * v)):
        ref[...] = val.astype(ref.dtype)


def _rwkv_pre(x, prev, g, mix, vec, mats, seq_len):
    m = x.shape[0]
    if prev is None:
        tm = _row_tile(seq_len, 256)
        seq_tiles = seq_len // tm
        p_arg = x
        p_spec = pl.BlockSpec((8, D_MODEL), lambda i: (jnp.maximum(i * (tm // 8) - 1, 0), 0))
        dtypes = [BF16, F32] + [BF16] * 6
    else:
        tm = _row_tile(m, 256)
        seq_tiles = 0
        p_arg = prev
        p_spec = pl.BlockSpec((tm, D_MODEL), lambda i: (i, 0))
        dtypes = [F32] * 8
    row_spec = pl.BlockSpec((tm, D_MODEL), lambda i: (i, 0))
    return pl.pallas_call(
        functools.partial(_rwkv_pre_kernel, seq_tiles=seq_tiles),
        grid=(m // tm,),
        in_specs=[row_spec, p_spec, _resident((1, D_MODEL)), _resident(mix.shape), _resident(vec.shape)]
                 + [_resident(w.shape) for w in mats],
        out_specs=[row_spec] * 8,
        out_shape=[jax.ShapeDtypeStruct((m, D_MODEL), dt) for dt in dtypes],
        compiler_params=_cparams(1), name="rwkv_pre",
    )(x, p_arg, g.reshape(1, D_MODEL), mix, vec, *mats)


def _wkv_kernel(r_ref, lw_ref, k_ref, v_ref, kk_ref, b_ref, gate_ref, bonus_ref, lnw_ref, lnb_ref,
                y_ref, sout_ref, s_ref, ar_ref, bk_ref, tinv_ref, aab_ref, rbk_ref, x0_ref, pc_ref,
                *, n_chunks):
    c_len = WKV_CHUNK
    n_pairs = RWKV_HEADS // 2
    pairs = range(n_pairs)
    t = pl.program_id(1)

    @pl.when(t == 0)
    def _():
        s_ref[...] = jnp.zeros(s_ref.shape, F32)

    rr = lax.broadcasted_iota(jnp.int32, (c_len, c_len), 0)
    cc = lax.broadcasted_iota(jnp.int32, (c_len, c_len), 1)
    ltri = (rr >= cc).astype(F32)
    r2 = lax.broadcasted_iota(jnp.int32, (LANES, LANES), 0)
    c2 = lax.broadcasted_iota(jnp.int32, (LANES, LANES), 1)
    strict = r2 > c2
    incl = r2 >= c2
    eye = (r2 == c2).astype(F32)
    ones2 = _pair_ones()
    first = lax.broadcasted_iota(jnp.int32, (c_len, LANES), 1) < RWKV_HEAD
    lanes = [slice(p * LANES, (p + 1) * LANES) for p in pairs]

    def stack(x):
        zero = jnp.zeros_like(x)
        return jnp.concatenate([jnp.where(first, x, zero), jnp.where(first, zero, x)], axis=0)

    def prepare(c):
        rows = pl.ds(c * c_len, c_len)
        lw = lw_ref[rows, :]
        cum = jnp.dot(ltri, lw, preferred_element_type=F32, precision=HIGHEST)
        e_in = jnp.exp(cum)
        e_neg = jnp.exp(-cum)
        at_all = (-(kk_ref[rows, :] * jnp.exp(cum - lw))).astype(BF16)
        rt_all = (r_ref[rows, :] * e_in).astype(BF16)
        bt_all = (b_ref[rows, :] * e_neg).astype(BF16)
        kt_all = (k_ref[rows, :] * e_neg).astype(BF16)
        v_all = v_ref[rows, :].astype(BF16)
        ar_ref[c] = jnp.concatenate([at_all, rt_all], axis=0)
        pc_ref[c] = e_in[c_len - 8:c_len, :]
        bk = [jnp.concatenate([stack(bt_all[:, sl]), stack(kt_all[:, sl])], axis=0) for sl in lanes]
        ar = [jnp.concatenate([stack(at_all[:, sl]), stack(rt_all[:, sl])], axis=0) for sl in lanes]
        gram = [_dot_nt(ar[p], bk[p]) for p in pairs]
        vs = [stack(v_all[:, sl]) for sl in lanes]
        a_ab = [jnp.where(strict, gram[p][:LANES, :LANES], 0.0) for p in pairs]
        a_ak = [jnp.where(strict, gram[p][:LANES, LANES:], 0.0).astype(BF16) for p in pairs]
        for p in pairs:
            bk_ref[c, p] = bk[p]
            both_incl = jnp.concatenate([incl, incl], axis=1)
            rbk_ref[c, p] = jnp.where(both_incl, gram[p][LANES:, :], 0.0).astype(BF16)
        x0 = [_dot(a_ak[p], vs[p]) for p in pairs]
        for p in pairs:
            x0_ref[c, p] = x0[p]
        tinv = [eye + a_ab[p] for p in pairs]
        pw = [a_ab[p].astype(BF16) for p in pairs]
        for p in pairs:
            aab_ref[c, p] = pw[p]
        pw = [_dot(pw[p], pw[p]).astype(BF16) for p in pairs]
        for _ in range(int(math.log2(c_len)) - 2):
            both = [_dot(pw[p], jnp.concatenate([tinv[p].astype(BF16), pw[p]], axis=1)) for p in pairs]
            tinv = [tinv[p] + both[p][:, :LANES] for p in pairs]
            pw = [both[p][:, LANES:].astype(BF16) for p in pairs]
        for p in pairs:
            tinv_ref[c, p] = (tinv[p] + _dot(pw[p], tinv[p].astype(BF16))).astype(BF16)

    def advance(c):
        rows = pl.ds(c * c_len, c_len)
        ar = ar_ref[c]
        s = [s_ref[p] for p in pairs]
        m1 = [_dot_nt(ar[:, lanes[p]], s[p].astype(BF16)) for p in pairs]
        x = [stack(m1[p][:c_len]) + x0_ref[c, p] for p in pairs]
        u = [_dot(tinv_ref[c, p], x[p].astype(BF16)) for p in pairs]
        for _ in range(WKV_REFINE_STEPS):
            res = [x[p] - u[p] + _dot(aab_ref[c, p], u[p].astype(BF16)) for p in pairs]
            u = [u[p] + _dot(tinv_ref[c, p], res[p].astype(BF16)) for p in pairs]
        uv = [jnp.concatenate([u[p].astype(BF16), stack(v_ref[rows, lanes[p]].astype(BF16))], axis=0)
              for p in pairs]
        y2 = [stack(m1[p][c_len:]) + _dot(rbk_ref[c, p], uv[p]) for p in pairs]
        upd = [_dot_tn(uv[p], bk_ref[c, p]) for p in pairs]
        decay = pc_ref[c]
        for p in pairs:
            s_ref[p] = (s[p] + upd[p]) * decay[7:8, lanes[p]]
        y = [y2[p][:c_len] + y2[p][c_len:] for p in pairs]
        mu = [_dot(y[p].astype(BF16), ones2) * (1.0 / RWKV_HEAD) for p in pairs]
        d = [y[p] - mu[p] for p in pairs]
        var = [_dot((d[p] * d[p]).astype(BF16), ones2) * (1.0 / RWKV_HEAD) for p in pairs]
        for p in pairs:
            sl = lanes[p]
            yn = d[p] * lax.rsqrt(var[p] + RWKV_GN_EPS) * lnw_ref[:, sl] + lnb_ref[:, sl]
            y_ref[rows, sl] = ((yn + bonus_ref[rows, sl]) * gate_ref[rows, sl]).astype(y_ref.dtype)

    prepare(0)
    for c in range(n_chunks):
        if c + 1 < n_chunks:
            prepare(c + 1)
        advance(c)

    @pl.when(t == pl.num_programs(1) - 1)
    def _():
        sout_ref[...] = s_ref[...]


def _wkv_prompt(streams, ln_w, ln_b, batch, seq_len):
    tw = _row_tile(seq_len, 256)
    nt = seq_len // tw
    nc = tw // WKV_CHUNK
    row_spec = pl.BlockSpec((tw, D_MODEL), lambda b, t: (b * nt + t, 0))
    n_pairs = RWKV_HEADS // 2
    pair_mat = lambda rows, cols, dt: pltpu.VMEM((nc, n_pairs, rows, cols), dt)
    return pl.pallas_call(
        functools.partial(_wkv_kernel, n_chunks=nc),
        grid=(batch, nt),
        in_specs=[row_spec] * 8 + [_resident((1, D_MODEL)), _resident((1, D_MODEL))],
        out_specs=[row_spec, pl.BlockSpec((None, n_pairs, LANES, LANES), lambda b, t: (b, 0, 0, 0))],
        out_shape=[jax.ShapeDtypeStruct((batch * seq_len, D_MODEL), BF16),
                   jax.ShapeDtypeStruct((batch, n_pairs, LANES, LANES), F32)],
        scratch_shapes=[pltpu.VMEM((n_pairs, LANES, LANES), F32),
                        pltpu.VMEM((nc, 2 * WKV_CHUNK, D_MODEL), BF16),
                        pair_mat(4 * WKV_CHUNK, LANES, BF16),
                        pair_mat(LANES, LANES, BF16), pair_mat(LANES, LANES, BF16),
                        pair_mat(LANES, 2 * LANES, BF16),
                        pair_mat(LANES, LANES, F32),
                        pltpu.VMEM((nc, 8, D_MODEL), F32)],
        compiler_params=_cparams(2), name="wkv_prompt",
    )(*streams, ln_w.reshape(1, D_MODEL), ln_b.reshape(1, D_MODEL))


def _wkv_step_kernel(s_ref, r_ref, lw_ref, k_ref, v_ref, kk_ref, b_ref, gate_ref, bonus_ref, lnw_ref, lnb_ref,
                     y_ref, sout_ref):
    n = RWKV_HEAD
    s = s_ref[...]
    eye = (lax.broadcasted_iota(jnp.int32, (n, n), 0) == lax.broadcasted_iota(jnp.int32, (n, n), 1)).astype(F32)
    sa = -jnp.sum(s * kk_ref[...], axis=-1, keepdims=True)
    vcol = jnp.sum(eye * v_ref[...], axis=-1, keepdims=True)
    s = s * jnp.exp(lw_ref[...]) + sa * b_ref[...] + vcol * k_ref[...]
    sout_ref[...] = s
    ycol = jnp.sum(s * r_ref[...], axis=-1, keepdims=True)
    y = jnp.sum(eye * ycol, axis=-2, keepdims=True)
    mu = jnp.mean(y, axis=-1, keepdims=True)
    d = y - mu
    var = jnp.mean(d * d, axis=-1, keepdims=True)
    yn = d * lax.rsqrt(var + RWKV_GN_EPS) * lnw_ref[...] + lnb_ref[...]
    y_ref[...] = (yn + bonus_ref[...]) * gate_ref[...]


def _wkv_step(state, streams, ln_w, ln_b):
    batch = state.shape[0]
    nb = 8 if batch % 8 == 0 else batch
    vec = lambda z: z.reshape(batch, RWKV_HEADS, 1, RWKV_HEAD)
    st_spec = pl.BlockSpec((nb, RWKV_HEADS, RWKV_HEAD, RWKV_HEAD), lambda i: (i, 0, 0, 0))
    vec_spec = pl.BlockSpec((nb, RWKV_HEADS, 1, RWKV_HEAD), lambda i: (i, 0, 0, 0))
    ln_spec = pl.BlockSpec((1, RWKV_HEADS, 1, RWKV_HEAD), lambda i: (0, 0, 0, 0))
    y, s_new = pl.pallas_call(
        _wkv_step_kernel,
        grid=(batch // nb,),
        in_specs=[st_spec] + [vec_spec] * 8 + [ln_spec, ln_spec],
        out_specs=[vec_spec, st_spec],
        out_shape=[jax.ShapeDtypeStruct((batch, RWKV_HEADS, 1, RWKV_HEAD), F32),
                   jax.ShapeDtypeStruct(state.shape, F32)],
        compiler_params=_cparams(1), name="wkv_step",
    )(state, *[vec(z) for z in streams], ln_w.reshape(1, RWKV_HEADS, 1, RWKV_HEAD),
      ln_b.reshape(1, RWKV_HEADS, 1, RWKV_HEAD))
    return y.reshape(batch, D_MODEL), s_new


CONV_HALO = 32
CONV_TAP_ROWS = 64


def _conv_glu_kernel(x_ref, g_ref, w_ref, b_ref, u_ref):
    h = _rms(x_ref[...], g_ref[...], NORM_EPS).astype(BF16)
    b = b_ref[...]
    a = _dot(h, w_ref[:, :D_MODEL]) + b[:, :D_MODEL]
    gate = _dot(h, w_ref[:, D_MODEL:]) + b[:, D_MODEL:]
    u_ref[...] = a * jax.nn.sigmoid(gate)


def _conv_glu(x, g, w1, b1):
    m = x.shape[0]
    tm = _row_tile(m, 512)
    row_spec = pl.BlockSpec((tm, D_MODEL), lambda i: (i, 0))
    return pl.pallas_call(
        _conv_glu_kernel,
        grid=(m // tm,),
        in_specs=[row_spec, _resident((1, D_MODEL)), _resident((D_MODEL, 2 * D_MODEL)), _resident((1, 2 * D_MODEL))],
        out_specs=row_spec,
        out_shape=jax.ShapeDtypeStruct((m, D_MODEL), F32),
        compiler_params=_cparams(1), name="conv_glu",
    )(x, g.reshape(1, D_MODEL), w1, b1.reshape(1, 2 * D_MODEL))


def _ln_silu(y, ln_w, ln_b):
    mu = jnp.mean(y, axis=-1, keepdims=True)
    d = y - mu
    var = jnp.mean(d * d, axis=-1, keepdims=True)
    yn = d * lax.rsqrt(var + LN_EPS) * ln_w + ln_b
    return yn * jax.nn.sigmoid(yn)


def _conv_dw_kernel(u_ref, halo_ref, dw_ref, vec_ref, w2_ref, res_ref, o_ref, ext_ref, s_ref, *, seq_tiles):
    tm = u_ref.shape[0]
    ext_ref[0:CONV_HALO, :] = jnp.where(pl.program_id(0) % seq_tiles == 0, 0.0, halo_ref[...])
    ext_ref[CONV_HALO:CONV_HALO + tm, :] = u_ref[...]
    vec = vec_ref[...]
    dw_b, ln_w, ln_b, b2 = (vec[c:c + 1] for c in range(4))
    dw = dw_ref[...]
    phases = [[] for _ in range(8)]
    for k in range(CONV_WIDTH):
        off = CONV_HALO - CONV_BUF + k
        phases[off % 8].append((k, off - off % 8))
    for c in range(tm // CONV_TAP_ROWS):
        base = c * CONV_TAP_ROWS
        acc = dw_b
        for b, taps in enumerate(phases):
            part = None
            n_rows = CONV_TAP_ROWS + (8 if b else 0)
            for k, off in taps:
                term = dw[k:k + 1] * ext_ref[base + off:base + off + n_rows, :]
                part = term if part is None else part + term
            acc = acc + part[b:b + CONV_TAP_ROWS]
        s_ref[base:base + CONV_TAP_ROWS, :] = _ln_silu(acc, ln_w, ln_b).astype(BF16)
    o_ref[...] = res_ref[...] + _dot(s_ref[...], w2_ref[...]) + b2


def _conv_dw(u, dw, vec, w2, res, seq_len):
    m = u.shape[0]
    tm = _row_tile(seq_len, 256)
    per = tm // CONV_HALO
    row_spec = pl.BlockSpec((tm, D_MODEL), lambda i: (i, 0))
    halo_spec = pl.BlockSpec((CONV_HALO, D_MODEL), lambda i: (jnp.maximum(i * per - 1, 0), 0))
    return pl.pallas_call(
        functools.partial(_conv_dw_kernel, seq_tiles=seq_len // tm),
        grid=(m // tm,),
        in_specs=[row_spec, halo_spec, _resident(dw.shape), _resident(vec.shape), _resident((D_MODEL, D_MODEL)), row_spec],
        out_specs=row_spec,
        out_shape=jax.ShapeDtypeStruct((m, D_MODEL), F32),
        scratch_shapes=[pltpu.VMEM((CONV_HALO + tm, D_MODEL), F32), pltpu.VMEM((tm, D_MODEL), BF16)],
        compiler_params=_cparams(1), name="conv_dw",
    )(u, u, dw, vec, w2, res)


def _conv_step_kernel(buf_ref, u_ref, dw_ref, vec_ref, s_ref):
    vec = vec_ref[...]
    dw = dw_ref[...]
    y = (jnp.sum(buf_ref[...] * dw[0:CONV_BUF], axis=1, keepdims=True)
         + u_ref[...] * dw[CONV_BUF:CONV_WIDTH] + vec[0:1])
    s_ref[...] = _ln_silu(y, vec[1:2], vec[2:3])


def _conv_step(buf, u, dw, vec):
    batch = buf.shape[0]
    nb = 8 if batch % 8 == 0 else batch
    one_spec = pl.BlockSpec((nb, 1, D_MODEL), lambda i: (i, 0, 0))
    return pl.pallas_call(
        _conv_step_kernel,
        grid=(batch // nb,),
        in_specs=[pl.BlockSpec((nb, CONV_BUF, D_MODEL), lambda i: (i, 0, 0)), one_spec,
                  _resident(dw.shape), _resident(vec.shape)],
        out_specs=one_spec,
        out_shape=jax.ShapeDtypeStruct((batch, 1, D_MODEL), F32),
        compiler_params=_cparams(1), name="conv_step",
    )(buf, u.reshape(batch, 1, D_MODEL), dw, vec).reshape(batch, D_MODEL)


def _rows8(*rows):
    pad = [jnp.zeros((D_MODEL,), F32)] * (8 - len(rows))
    return jnp.stack([r.reshape(D_MODEL).astype(F32) for r in rows] + pad)


def kernel(x_prompt, x_sample, cache_k, cache_v, page_table, state_wkv, state_shift, state_conv, norm_mix, norm_ffn, norm_final, attn_w_qkv, attn_w_o, attn_lambda, attn_subln, rwkv_mix, rwkv_w0, rwkv_w1, rwkv_w2, rwkv_a0, rwkv_a1, rwkv_a2, rwkv_g1, rwkv_g2, rwkv_k_k, rwkv_k_a, rwkv_r_k, rwkv_wr, rwkv_wk, rwkv_wv, rwkv_wo, rwkv_ln_w, rwkv_ln_b, conv_w1, conv_b1, conv_dw, conv_dw_b, conv_ln_w, conv_ln_b, conv_w2, conv_b2, ffn_w_up, ffn_w_down):
    bp, seq, d = x_prompt.shape
    bs = x_sample.shape[0]
    n_layers_attn, n_pool, page, heads, vdim = cache_k.shape
    past_len = page_table.shape[1] * page
    xp = x_prompt.reshape(bp * seq, d)
    xs = x_sample.reshape(bs, d)
    tabs_p = _rope_tables(jnp.arange(seq, dtype=jnp.int32))
    tabs_s = _rope_tables(jnp.full((bs,), past_len, jnp.int32))
    no_bias = jnp.zeros((d,), F32)
    kv_p = kv_s = n_layers_attn
    wkv_p, wkv_s, sh_p, sh_s, cv_p, cv_s = ([] for _ in range(6))
    for i in range(DEPTH):
        j = i // N_MIXERS
        kind = i % N_MIXERS
        if kind == 0:
            lam_init = 0.8 - 0.6 * math.exp(-0.3 * i)
            wqkv = attn_w_qkv[j].astype(BF16)
            wo = attn_w_o[j].astype(BF16)
            q, kb, vb, kv_p = _attn_qkv(xp, norm_mix[i], wqkv, tabs_p, seq, j, kv_p)
            o = _attn_prompt(q, kb, vb, attn_lambda[j], attn_subln[j], lam_init, bp, seq)
            mix_p = (o, wo, no_bias)
            q_s, _, _, kv_s = _attn_qkv(xs, norm_mix[i], wqkv, tabs_s, bs, j, kv_s)
            tok = lambda z: z.reshape(bs, heads, vdim)
            o_s = _attn_decode(tok(q_s.astype(F32)), tok(kv_s[0][j]), tok(kv_s[1][j]), cache_k, cache_v, page_table,
                               j, attn_lambda[j], attn_subln[j], lam_init)
            mix_s = (o_s.reshape(bs, d), wo, no_bias)
        elif kind == 1:
            mats = [w[j].astype(BF16) for w in (rwkv_wr, rwkv_wk, rwkv_wv, rwkv_w1, rwkv_w2, rwkv_a1, rwkv_a2,
                                                rwkv_g1, rwkv_g2)]
            vec = _rows8(rwkv_w0[j], rwkv_a0[j], rwkv_k_k[j], rwkv_k_a[j], rwkv_r_k[j])
            wo = rwkv_wo[j].astype(BF16)
            sh_p.append(_norm_rows(xp.reshape(bp, seq, d)[:, -1], norm_mix[i]))
            sh_s.append(_norm_rows(xs, norm_mix[i]))
            streams = _rwkv_pre(xp, None, norm_mix[i], rwkv_mix[j], vec, mats, seq)
            y, s_pairs = _wkv_prompt(streams, rwkv_ln_w[j], rwkv_ln_b[j], bp, seq)
            mix_p = (y, wo, no_bias)
            s6 = s_pairs.reshape(bp, RWKV_HEADS // 2, 2, RWKV_HEAD, 2, RWKV_HEAD)
            wkv_p.append(jnp.stack([s6[:, :, 0, :, 0, :], s6[:, :, 1, :, 1, :]], axis=2)
                         .reshape(bp, RWKV_HEADS, RWKV_HEAD, RWKV_HEAD))
            streams = _rwkv_pre(xs, state_shift[j], norm_mix[i], rwkv_mix[j], vec, mats, 1)
            y_s, s_new = _wkv_step(state_wkv[j], streams, rwkv_ln_w[j], rwkv_ln_b[j])
            mix_s = (y_s, wo, no_bias)
            wkv_s.append(s_new)
        else:
            w1 = conv_w1[j].astype(BF16)
            w2 = conv_w2[j].astype(BF16)
            dw = jnp.concatenate([conv_dw[j], jnp.zeros((1, d), F32)], axis=0)
            vec = _rows8(conv_dw_b[j], conv_ln_w[j], conv_ln_b[j], conv_b2[j])
            u = _conv_glu(xp, norm_mix[i], w1, conv_b1[j])
            xp = _conv_dw(u, dw, vec, w2, xp, seq)
            cv_p.append(u.reshape(bp, seq, d)[:, seq - CONV_BUF:])
            u_s = _conv_glu(xs, norm_mix[i], w1, conv_b1[j])
            mix_p = None
            mix_s = (_conv_step(state_conv[j], u_s, dw, vec), w2, conv_b2[j])
            cv_s.append(jnp.concatenate([state_conv[j][:, 1:], u_s[:, None, :]], axis=1))
        wup = ffn_w_up[i].astype(BF16)
        wdn = ffn_w_down[i].astype(BF16)
        out_norm = norm_final if i == DEPTH - 1 else None
        xp = _ffn(xp, norm_ffn[i], wup, wdn, mix_p, out_norm)
        xs = _ffn(xs, norm_ffn[i], wup, wdn, mix_s, out_norm)
    return (xp.reshape(bp, seq, d), xs.reshape(bs, 1, d),
            kv_p[0].reshape(n_layers_attn, bp, seq, heads, vdim), kv_p[1].reshape(n_layers_attn, bp, seq, heads, vdim),
            kv_s[0].reshape(n_layers_attn, bs, 1, heads, vdim), kv_s[1].reshape(n_layers_attn, bs, 1, heads, vdim),
            jnp.stack(wkv_p), jnp.stack(wkv_s),
            jnp.stack(sh_p), jnp.stack(sh_s), jnp.stack(cv_p), jnp.stack(cv_s))
```
